```python
import math
import jax, jax.numpy as jnp
from jax import lax
import numpy as np

D_MODEL = 1024
BATCH = 8
SEQ = 8192
DEPTH = 4

N_MEM = 256
N_MIXERS = 2
N_S5 = (DEPTH + 1) // 2
N_RET = DEPTH // 2

S5_GROUP = 16
S5_GROUPS = D_MODEL // S5_GROUP
S5_STATE = 64
DT_MIN = 1e-3
DT_MAX = 1e-1

RET_HEADS = 4
RET_QK_DIM = D_MODEL // RET_HEADS
RET_V_DIM = 2 * RET_QK_DIM
RET_VALUE_WIDTH = RET_HEADS * RET_V_DIM
RET_IN_WIDTH = 2 * D_MODEL + 2 * RET_VALUE_WIDTH
RET_CHUNK = 128
ROPE_BASE = 10000.0

XATTN_HEADS = 4
XATTN_DIM = D_MODEL // XATTN_HEADS

D_FF = 2816
MACARON_WEIGHT = 0.5
EPS = 1e-6

kernel_name = "hybrid_s5_retention_macaron_memxattn"


def _rmsnorm(x, g):
    xf = x.astype(jnp.float32)
    y = xf * lax.rsqrt(jnp.mean(xf * xf, axis=-1, keepdims=True) + EPS) * g.astype(jnp.float32)
    return y.astype(x.dtype)


def _swiglu(x, w_in, w_out):
    a, b = jnp.split(x @ w_in, 2, axis=-1)
    return (jax.nn.silu(a) * b) @ w_out


def _complex_linear_combine(e1, e2):
    a1r, a1i, b1r, b1i = e1
    a2r, a2i, b2r, b2i = e2
    ar = a2r * a1r - a2i * a1i
    ai = a2r * a1i + a2i * a1r
    br = a2r * b1r - a2i * b1i + b2r
    bi = a2r * b1i + a2i * b1r + b2i
    return ar, ai, br, bi


def _s5_mixer(u, lam_re, lam_im, log_dt, b_re, b_im, c_re, c_im, d_skip, w_glu):
    bsz, seq, _ = u.shape
    f32 = jnp.float32
    uf = u.astype(f32)
    lr = lam_re.astype(f32)
    li = lam_im.astype(f32)
    dt = jnp.exp(log_dt.astype(f32))[:, None]
    mag = jnp.exp(lr * dt)
    ar = mag * jnp.cos(li * dt)
    ai = mag * jnp.sin(li * dt)
    den = lr * lr + li * li
    nr = ar - 1.0
    zr = (nr * lr + ai * li) / den
    zi = (ai * lr - nr * li) / den
    br = b_re.astype(f32)
    bi = b_im.astype(f32)
    bbar_re = zr[..., None] * br - zi[..., None] * bi
    bbar_im = zr[..., None] * bi + zi[..., None] * br
    ug = uf.reshape(bsz, seq, S5_GROUPS, S5_GROUP)
    bu_re = jnp.einsum('bsgh,gph->sbgp', ug, bbar_re)
    bu_im = jnp.einsum('bsgh,gph->sbgp', ug, bbar_im)
    a_re = jnp.broadcast_to(ar, (seq, 1) + ar.shape)
    a_im = jnp.broadcast_to(ai, (seq, 1) + ai.shape)
    _, _, xr, xi = lax.associative_scan(_complex_linear_combine, (a_re, a_im, bu_re, bu_im), axis=0)
    y = (jnp.einsum('sbgp,ghp->bsgh', xr, c_re.astype(f32))
         - jnp.einsum('sbgp,ghp->bsgh', xi, c_im.astype(f32)))
    y = y.reshape(bsz, seq, D_MODEL) + d_skip.astype(f32) * uf
    y = jax.nn.gelu(y).astype(u.dtype)
    a, gate = jnp.split(y @ w_glu, 2, axis=-1)
    return a * jax.nn.sigmoid(gate)


def _rotary(t, cos, sin):
    t1, t2 = jnp.split(t, 2, axis=-1)
    return jnp.concatenate([t1 * cos - t2 * sin, t1 * sin + t2 * cos], axis=-1)


def _retention_mixer(u, positions, w_in, w_out):
    bsz, seq, _ = u.shape
    f32 = jnp.float32
    h = u @ w_in
    q, k, v, g = jnp.split(h, [D_MODEL, 2 * D_MODEL, 2 * D_MODEL + RET_VALUE_WIDTH], axis=-1)
    q = q.astype(f32).reshape(bsz, seq, RET_HEADS, RET_QK_DIM)
    k = k.astype(f32).reshape(bsz, seq, RET_HEADS, RET_QK_DIM) * (RET_QK_DIM ** -0.5)
    v = v.astype(f32).reshape(bsz, seq, RET_HEADS, RET_V_DIM)
    half = RET_QK_DIM // 2
    inv_freq = 1.0 / (ROPE_BASE ** jnp.linspace(0.0, 1.0, half, dtype=f32))
    ang = positions.astype(f32)[..., None] * inv_freq
    cos = jnp.cos(ang)[:, :, None, :]
    sin = jnp.sin(ang)[:, :, None, :]
    q = _rotary(q, cos, sin)
    k = _rotary(k, cos, sin)
    gamma = 1.0 - jnp.exp2(-5.0 - jnp.arange(RET_HEADS, dtype=f32))
    lg = jnp.log(gamma)
    idx = jnp.arange(RET_CHUNK)
    diff = idx[:, None] - idx[None, :]
    d_mat = jnp.where(diff >= 0, jnp.exp(lg[:, None, None] * jnp.maximum(diff, 0).astype(f32)), 0.0)
    xi = jnp.exp(lg[:, None] * (idx + 1).astype(f32))
    zeta = jnp.exp(lg[:, None] * (RET_CHUNK - 1 - idx).astype(f32))
    gamma_chunk = jnp.exp(lg * RET_CHUNK)
    n_chunks = seq // RET_CHUNK

    def to_chunks(t):
        d = t.shape[-1]
        return t.reshape(bsz, n_chunks, RET_CHUNK, RET_HEADS, d).transpose(1, 0, 3, 2, 4)

    def step(state, inp):
        qc, kc, vc = inp
        inner = jnp.einsum('bhid,bhjd->bhij', qc, kc) * d_mat
        out = (jnp.einsum('bhij,bhjv->bhiv', inner, vc)
               + jnp.einsum('bhid,bhdv->bhiv', qc, state) * xi[None, :, :, None])
        state = (state * gamma_chunk[None, :, None, None]
                 + jnp.einsum('bhjd,bhjv->bhdv', kc * zeta[None, :, :, None], vc))
        return state, out

    state0 = jnp.zeros((bsz, RET_HEADS, RET_QK_DIM, RET_V_DIM), f32)
    _, outs = lax.scan(step, state0, (to_chunks(q), to_chunks(k), to_chunks(v)))
    o = outs.transpose(1, 0, 3, 2, 4).reshape(bsz, seq, RET_HEADS, RET_V_DIM)
    mu = jnp.mean(o, axis=-1, keepdims=True)
    var = jnp.mean(jnp.square(o - mu), axis=-1, keepdims=True)
    o = ((o - mu) * lax.rsqrt(var + EPS)).reshape(bsz, seq, RET_VALUE_WIDTH)
    y = (jax.nn.silu(g.astype(f32)) * o).astype(u.dtype)
    return y @ w_out


def _memory_cross_attention(xn, mem_n, w_q, w_kv, w_o):
    bsz, seq, _ = xn.shape
    q = (xn @ w_q).reshape(bsz, seq, XATTN_HEADS, XATTN_DIM)
    k, v = jnp.split(mem_n @ w_kv, 2, axis=-1)
    k = k.reshape(bsz, N_MEM, XATTN_HEADS, XATTN_DIM)
    v = v.reshape(bsz, N_MEM, XATTN_HEADS, XATTN_DIM)
    s = jnp.einsum('bshd,bmhd->bhsm', q.astype(jnp.float32), k.astype(jnp.float32)) * (XATTN_DIM ** -0.5)
    p = jax.nn.softmax(s, axis=-1).astype(xn.dtype)
    o = jnp.einsum('bhsm,bmhd->bshd', p, v).reshape(bsz, seq, D_MODEL)
    return o @ w_o


def setup_inputs(seed: int = 0) -> dict:
    key = jax.random.key(seed)
    ks = jax.random.split(key, 32)
    f32 = jnp.float32
    nrm = lambda k, shape, scale: jax.random.normal(k, shape, f32) * scale
    x = nrm(ks[0], (BATCH, SEQ, D_MODEL), 1.0)
    mem = nrm(ks[1], (BATCH, N_MEM, D_MODEL), 1.0)
    offset = jax.random.randint(ks[2], (BATCH, 1), 0, 4096, dtype=jnp.int32)
    positions = offset + jnp.arange(SEQ, dtype=jnp.int32)[None, :]
    norm_gains = 1.0 + nrm(ks[3], (DEPTH, 4, D_MODEL), 0.02)
    mem_norm = 1.0 + nrm(ks[4], (D_MODEL,), 0.02)
    final_norm = 1.0 + nrm(ks[5], (D_MODEL,), 0.02)
    ffn1_w_in = nrm(ks[6], (DEPTH, D_MODEL, 2 * D_FF), D_MODEL ** -0.5)
    ffn1_w_out = nrm(ks[7], (DEPTH, D_FF, D_MODEL), D_FF ** -0.5)
    ffn2_w_in = nrm(ks[8], (DEPTH, D_MODEL, 2 * D_FF), D_MODEL ** -0.5)
    ffn2_w_out = nrm(ks[9], (DEPTH, D_FF, D_MODEL), D_FF ** -0.5)
    n_idx = jnp.arange(S5_STATE, dtype=f32)
    s5_lam_re = -0.5 + nrm(ks[10], (N_S5, S5_GROUPS, S5_STATE), 0.01)
    s5_lam_im = math.pi * n_idx[None, None, :] + nrm(ks[11], (N_S5, S5_GROUPS, S5_STATE), 0.01)
    s5_log_dt = jax.random.uniform(ks[12], (N_S5, S5_GROUPS), f32, math.log(DT_MIN), math.log(DT_MAX))
    s5_b_re = nrm(ks[13], (N_S5, S5_GROUPS, S5_STATE, S5_GROUP), (2 * S5_GROUP) ** -0.5)
    s5_b_im = nrm(ks[14], (N_S5, S5_GROUPS, S5_STATE, S5_GROUP), (2 * S5_GROUP) ** -0.5)
    s5_c_re = nrm(ks[15], (N_S5, S5_GROUPS, S5_GROUP, S5_STATE), S5_STATE ** -0.5)
    s5_c_im = nrm(ks[16], (N_S5, S5_GROUPS, S5_GROUP, S5_STATE), S5_STATE ** -0.5)
    s5_d = nrm(ks[17], (N_S5, D_MODEL), 1.0)
    s5_w_glu = nrm(ks[18], (N_S5, D_MODEL, 2 * D_MODEL), D_MODEL ** -0.5)
    ret_w_in = nrm(ks[19], (N_RET, D_MODEL, RET_IN_WIDTH), D_MODEL ** -0.5)
    ret_w_out = nrm(ks[20], (N_RET, RET_VALUE_WIDTH, D_MODEL), RET_VALUE_WIDTH ** -0.5)
    xattn_w_q = nrm(ks[21], (DEPTH, D_MODEL, D_MODEL), D_MODEL ** -0.5)
    xattn_w_kv = nrm(ks[22], (DEPTH, D_MODEL, 2 * D_MODEL), D_MODEL ** -0.5)
    xattn_w_o = nrm(ks[23], (DEPTH, D_MODEL, D_MODEL), D_MODEL ** -0.5)
    return {
        "x": x, "mem": mem, "positions": positions,
        "norm_gains": norm_gains, "mem_norm": mem_norm, "final_norm": final_norm,
        "ffn1_w_in": ffn1_w_in, "ffn1_w_out": ffn1_w_out,
        "ffn2_w_in": ffn2_w_in, "ffn2_w_out": ffn2_w_out,
        "s5_lam_re": s5_lam_re, "s5_lam_im": s5_lam_im, "s5_log_dt": s5_log_dt,
        "s5_b_re": s5_b_re, "s5_b_im": s5_b_im, "s5_c_re": s5_c_re, "s5_c_im": s5_c_im,
        "s5_d": s5_d, "s5_w_glu": s5_w_glu,
        "ret_w_in": ret_w_in, "ret_w_out": ret_w_out,
        "xattn_w_q": xattn_w_q, "xattn_w_kv": xattn_w_kv, "xattn_w_o": xattn_w_o,
    }


def reference(x, mem, positions, norm_gains, mem_norm, final_norm,
              ffn1_w_in, ffn1_w_out, ffn2_w_in, ffn2_w_out,
              s5_lam_re, s5_lam_im, s5_log_dt, s5_b_re, s5_b_im, s5_c_re, s5_c_im, s5_d, s5_w_glu,
              ret_w_in, ret_w_out, xattn_w_q, xattn_w_kv, xattn_w_o):
    mem_n = _rmsnorm(mem, mem_norm)
    for i in range(DEPTH):
        g = norm_gains[i]
        x = x + MACARON_WEIGHT * _swiglu(_rmsnorm(x, g[0]), ffn1_w_in[i], ffn1_w_out[i])
        h = _rmsnorm(x, g[1])
        j = i // N_MIXERS
        if i % N_MIXERS == 0:
            mix = _s5_mixer(h, s5_lam_re[j], s5_lam_im[j], s5_log_dt[j], s5_b_re[j], s5_b_im[j],
                            s5_c_re[j], s5_c_im[j], s5_d[j], s5_w_glu[j])
        else:
            mix = _retention_mixer(h, positions, ret_w_in[j], ret_w_out[j])
        x = x + mix
        x = x + _memory_cross_attention(_rmsnorm(x, g[2]), mem_n, xattn_w_q[i], xattn_w_kv[i], xattn_w_o[i])
        x = x + MACARON_WEIGHT * _swiglu(_rmsnorm(x, g[3]), ffn2_w_in[i], ffn2_w_out[i])
    return _rmsnorm(x, final_norm)
```

```python
import functools
import math

import jax
import jax.numpy as jnp
from jax import lax
from jax.experimental import pallas as pl
from jax.experimental.pallas import tpu as pltpu

F32 = jnp.float32
BF16 = jnp.bfloat16

D_MODEL = 1024
DEPTH = 4
N_MEM = 256
D_FF = 2816
MACARON_WEIGHT = 0.5
EPS = 1e-6

S5_GROUP = 16
S5_GROUPS = D_MODEL // S5_GROUP
S5_STATE = 64
S5_CHUNK = 128
S5_COLS = S5_GROUP * S5_CHUNK

RET_HEADS = 4
RET_QK_DIM = D_MODEL // RET_HEADS
RET_V_DIM = 2 * RET_QK_DIM
RET_VALUE_WIDTH = RET_HEADS * RET_V_DIM
RET_IN_WIDTH = 2 * D_MODEL + 2 * RET_VALUE_WIDTH
RET_CHUNK = 256
ROPE_BASE = 10000.0
ROPE_HALF = RET_QK_DIM // 2

XATTN_HEADS = 4
XATTN_DIM = D_MODEL // XATTN_HEADS

TOKEN_TILE = 512
VMEM_LIMIT = 56 * 1024 * 1024


def _params(n_axes):
    return pltpu.CompilerParams(dimension_semantics=("arbitrary",) * n_axes,
                                vmem_limit_bytes=VMEM_LIMIT)


def _resident(shape):
    zeros = (0,) * len(shape)
    return pl.BlockSpec(shape, lambda *_: zeros, pipeline_mode=pl.Buffered(1))


def _rmsnorm(x, g):
    return x * lax.rsqrt(jnp.mean(x * x, axis=-1, keepdims=True) + EPS) * g


def _dot(a, b):
    return jnp.dot(a, b, preferred_element_type=F32)


def _ffn_kernel(x_ref, g_ref, w_in_ref, w_out_ref, gf_ref, o_ref, *, final_norm):
    x = x_ref[...]
    xn = _rmsnorm(x, g_ref[...]).astype(BF16)
    h = _dot(xn, w_in_ref[...])
    a = h[:, :D_FF]
    act = (a * jax.nn.sigmoid(a) * h[:, D_FF:]).astype(BF16)
    y = x + MACARON_WEIGHT * _dot(act, w_out_ref[...])
    if final_norm:
        y = _rmsnorm(y, gf_ref[...])
    o_ref[...] = y


def _ffn(x2d, g, w_in, w_out, gf, final_norm):
    tokens = x2d.shape[0]
    row = pl.BlockSpec((TOKEN_TILE, D_MODEL), lambda i: (i, 0))
    return pl.pallas_call(
        functools.partial(_ffn_kernel, final_norm=final_norm),
        grid=(tokens // TOKEN_TILE,),
        in_specs=[row, _resident((1, D_MODEL)), _resident((D_MODEL, 2 * D_FF)),
                  _resident((D_FF, D_MODEL)), _resident((1, D_MODEL))],
        out_specs=row,
        out_shape=jax.ShapeDtypeStruct((tokens, D_MODEL), F32),
        compiler_params=_params(1),
        name="ffn",
    )(x2d, g, w_in, w_out, gf)


def _kv_kernel(mem_ref, g_ref, w_ref, o_ref):
    mn = _rmsnorm(mem_ref[...], g_ref[...]).astype(BF16)
    o_ref[0] = _dot(mn, w_ref[0]).astype(BF16)


def _memory_kv(mem2d, g, w_kv):
    rows = mem2d.shape[0]
    half = rows // 2
    return pl.pallas_call(
        _kv_kernel,
        grid=(DEPTH, 2),
        in_specs=[pl.BlockSpec((half, D_MODEL), lambda l, r: (r, 0)),
                  _resident((1, D_MODEL)),
                  pl.BlockSpec((1, D_MODEL, 2 * D_MODEL), lambda l, r: (l, 0, 0))],
        out_specs=pl.BlockSpec((1, half, 2 * D_MODEL), lambda l, r: (l, r, 0)),
        out_shape=jax.ShapeDtypeStruct((DEPTH, rows, 2 * D_MODEL), BF16),
        compiler_params=_params(2),
        name="memory_kv",
    )(mem2d, g, w_kv)


def _xattn_kernel(x_ref, g_ref, wq_ref, k_ref, v_ref, wo_ref, o_ref):
    x = x_ref[0]
    xn = _rmsnorm(x, g_ref[...]).astype(BF16)
    q = (_dot(xn, wq_ref[...]) * (XATTN_DIM ** -0.5)).astype(BF16)
    heads = []
    for h in range(XATTN_HEADS):
        sl = slice(h * XATTN_DIM, (h + 1) * XATTN_DIM)
        s = lax.dot_general(q[:, sl], k_ref[0, :, sl], (((1,), (1,)), ((), ())),
                            preferred_element_type=F32)
        e = jnp.exp(s - jnp.max(s, axis=-1, keepdims=True))
        p = e / jnp.sum(e, axis=-1, keepdims=True)
        heads.append(_dot(p.astype(BF16), v_ref[0, :, sl]).astype(BF16))
    o = jnp.concatenate(heads, axis=-1)
    o_ref[0] = x + _dot(o, wo_ref[...])


def _xattn(x3d, g, w_q, kv, w_o):
    bsz, seq, _ = x3d.shape
    row = pl.BlockSpec((1, TOKEN_TILE, D_MODEL), lambda b, i: (b, i, 0))
    return pl.pallas_call(
        _xattn_kernel,
        grid=(bsz, seq // TOKEN_TILE),
        in_specs=[row, _resident((1, D_MODEL)), _resident((D_MODEL, D_MODEL)),
                  pl.BlockSpec((1, N_MEM, D_MODEL), lambda b, i: (b, 0, 0)),
                  pl.BlockSpec((1, N_MEM, D_MODEL), lambda b, i: (b, 0, 1)),
                  _resident((D_MODEL, D_MODEL))],
        out_specs=row,
        out_shape=jax.ShapeDtypeStruct(x3d.shape, F32),
        compiler_params=_params(2),
        name="xattn",
    )(x3d, g, w_q, kv, kv, w_o)


def _ret_in_kernel(x_ref, g_ref, pos_ref, freq_ref, w_ref, q_ref, k_ref, v_ref, gate_ref):
    xn = _rmsnorm(x_ref[...], g_ref[...]).astype(BF16)
    h = _dot(xn, w_ref[...])
    ang = pos_ref[...] * freq_ref[...]
    cos = jnp.cos(ang)
    sin = jnp.sin(ang)
    k_scale = RET_QK_DIM ** -0.5
    for hd in range(RET_HEADS):
        lo = hd * RET_QK_DIM
        mid = lo + ROPE_HALF
        hi = lo + RET_QK_DIM
        q1, q2 = h[:, lo:mid], h[:, mid:hi]
        q_ref[:, lo:mid] = (q1 * cos - q2 * sin).astype(BF16)
        q_ref[:, mid:hi] = (q1 * sin + q2 * cos).astype(BF16)
        k1 = h[:, D_MODEL + lo:D_MODEL + mid] * k_scale
        k2 = h[:, D_MODEL + mid:D_MODEL + hi] * k_scale
        k_ref[:, lo:mid] = (k1 * cos - k2 * sin).astype(BF16)
        k_ref[:, mid:hi] = (k1 * sin + k2 * cos).astype(BF16)
    v_ref[...] = h[:, 2 * D_MODEL:2 * D_MODEL + RET_VALUE_WIDTH].astype(BF16)
    gate_ref[...] = h[:, 2 * D_MODEL + RET_VALUE_WIDTH:].astype(BF16)


def _ret_in(x2d, g, pos_lanes, inv_freq, w_in):
    tokens = x2d.shape[0]

    def row(width):
        return pl.BlockSpec((TOKEN_TILE, width), lambda i: (i, 0))

    def out(width):
        return jax.ShapeDtypeStruct((tokens, width), BF16)

    return pl.pallas_call(
        _ret_in_kernel,
        grid=(tokens // TOKEN_TILE,),
        in_specs=[row(D_MODEL), _resident((1, D_MODEL)), row(ROPE_HALF),
                  _resident((1, ROPE_HALF)), _resident((D_MODEL, RET_IN_WIDTH))],
        out_specs=[row(D_MODEL), row(D_MODEL), row(RET_VALUE_WIDTH), row(RET_VALUE_WIDTH)],
        out_shape=[out(D_MODEL), out(D_MODEL), out(RET_VALUE_WIDTH), out(RET_VALUE_WIDTH)],
        compiler_params=_params(1),
        name="ret_in",
    )(x2d, g, pos_lanes, inv_freq, w_in)


def _ret_core_kernel(x_ref, q_ref, k_ref, v_ref, gate_ref, w_out_ref, o_ref, state_ref, y_ref):
    @pl.when(pl.program_id(1) == 0)
    def _():
        state_ref[...] = jnp.zeros_like(state_ref)

    c = RET_CHUNK
    row_qk = lax.broadcasted_iota(jnp.int32, (c, RET_QK_DIM), 0).astype(F32)
    row_v = lax.broadcasted_iota(jnp.int32, (c, RET_V_DIM), 0).astype(F32)
    diff = (lax.broadcasted_iota(jnp.int32, (c, c), 0)
            - lax.broadcasted_iota(jnp.int32, (c, c), 1))
    lag = jnp.maximum(diff, 0).astype(F32)
    for hd in range(RET_HEADS):
        log_gamma = math.log(1.0 - 2.0 ** (-5 - hd))
        d_mat = jnp.where(diff >= 0, jnp.exp(log_gamma * lag), 0.0)
        xi = jnp.exp(log_gamma * (row_v + 1.0))
        zeta = jnp.exp(log_gamma * (c - 1.0 - row_qk))
        gamma_chunk = math.exp(log_gamma * c)
        qk = slice(hd * RET_QK_DIM, (hd + 1) * RET_QK_DIM)
        vv = slice(hd * RET_V_DIM, (hd + 1) * RET_V_DIM)
        for ci in range(TOKEN_TILE // c):
            rows = slice(ci * c, (ci + 1) * c)
            qc = q_ref[0, rows, qk]
            kc = k_ref[0, rows, qk]
            vc = v_ref[0, rows, vv]
            state = state_ref[hd]
            inner = lax.dot_general(qc, kc, (((1,), (1,)), ((), ())),
                                    preferred_element_type=F32) * d_mat
            o = _dot(inner.astype(BF16), vc) + _dot(qc, state.astype(BF16)) * xi
            kz = (kc.astype(F32) * zeta).astype(BF16)
            state_ref[hd] = state * gamma_chunk + lax.dot_general(
                kz, vc, (((0,), (0,)), ((), ())), preferred_element_type=F32)
            mu = jnp.mean(o, axis=-1, keepdims=True)
            dev = o - mu
            var = jnp.mean(dev * dev, axis=-1, keepdims=True)
            gate = gate_ref[0, rows, vv].astype(F32)
            y = gate * jax.nn.sigmoid(gate) * (dev * lax.rsqrt(var + EPS))
            y_ref[rows, vv] = y.astype(BF16)
    o_ref[0] = x_ref[0] + _dot(y_ref[...], w_out_ref[...])


def _ret_core(x3d, q, k, v, gate, w_out):
    bsz, seq, _ = x3d.shape

    def row(width):
        return pl.BlockSpec((1, TOKEN_TILE, width), lambda b, i: (b, i, 0))

    return pl.pallas_call(
        _ret_core_kernel,
        grid=(bsz, seq // TOKEN_TILE),
        in_specs=[row(D_MODEL), row(D_MODEL), row(D_MODEL), row(RET_VALUE_WIDTH),
                  row(RET_VALUE_WIDTH), _resident((RET_VALUE_WIDTH, D_MODEL))],
        out_specs=row(D_MODEL),
        out_shape=jax.ShapeDtypeStruct(x3d.shape, F32),
        scratch_shapes=[pltpu.VMEM((RET_HEADS, RET_QK_DIM, RET_V_DIM), F32),
                        pltpu.VMEM((TOKEN_TILE, RET_VALUE_WIDTH), BF16)],
        compiler_params=_params(2),
        name="ret_core",
    )(x3d, q, k, v, gate, w_out)


def _s5_operators(lam_re, lam_im, log_dt, b_re, b_im, c_re, c_im):
    n = S5_CHUNK
    hp = lax.Precision.HIGHEST
    lr, li = lam_re.astype(F32), lam_im.astype(F32)
    dt = jnp.exp(log_dt.astype(F32))[:, None]
    mag = jnp.exp(lr * dt)
    ar = mag * jnp.cos(li * dt)
    ai = mag * jnp.sin(li * dt)
    den = lr * lr + li * li
    nr = ar - 1.0
    zr = (nr * lr + ai * li) / den
    zi = (ai * lr - nr * li) / den
    bbr = zr[..., None] * b_re - zi[..., None] * b_im
    bbi = zr[..., None] * b_im + zi[..., None] * b_re
    steps = jnp.arange(n + 1, dtype=F32)[None, :, None]
    pmag = jnp.exp((lr * dt)[:, None, :] * steps)
    phase = (li * dt)[:, None, :] * steps
    pr = pmag * jnp.cos(phase)
    pi = pmag * jnp.sin(phase)
    clr = c_re[:, :, None, :] * pr[:, None] - c_im[:, :, None, :] * pi[:, None]
    cli = c_re[:, :, None, :] * pi[:, None] + c_im[:, :, None, :] * pr[:, None]
    resp = (jnp.einsum('gotp,gph->ghot', clr[:, :, :n], bbr, precision=hp)
            - jnp.einsum('gotp,gph->ghot', cli[:, :, :n], bbi, precision=hp)).astype(BF16)
    src = jnp.arange(n)[:, None]
    dst = jnp.arange(n)[None, :]
    lag = dst - src
    toep = jnp.where(lag >= 0, jnp.take(resp, jnp.maximum(lag, 0), axis=-1), 0)
    toep = toep.transpose(0, 1, 3, 2, 4).reshape(S5_GROUPS, S5_COLS, S5_COLS)
    prr = pr[:, n - 1::-1][:, None]
    pir = pi[:, n - 1::-1][:, None]
    bbr_t = bbr.transpose(0, 2, 1)[:, :, None, :]
    bbi_t = bbi.transpose(0, 2, 1)[:, :, None, :]
    wr = prr * bbr_t - pir * bbi_t
    wi = prr * bbi_t + pir * bbr_t
    w = jnp.concatenate([wr, wi, wi, wr], axis=-1).reshape(S5_GROUPS, S5_COLS, 4 * S5_STATE)
    zr_ = clr[:, :, 1:].transpose(0, 3, 1, 2)
    zi_ = -cli[:, :, 1:].transpose(0, 3, 1, 2)
    z = jnp.concatenate([zr_, zi_], axis=1).reshape(S5_GROUPS, 2 * S5_STATE, S5_COLS)
    alr, ali = pr[:, n], pi[:, n]
    zero = jnp.zeros_like(alr)
    a = jnp.stack([jnp.concatenate([alr, alr], -1), jnp.concatenate([-ali, ali], -1),
                   jnp.concatenate([ali, -ali], -1)] + [jnp.concatenate([zero, zero], -1)] * 5,
                  axis=1)
    return toep, w.astype(BF16), z.astype(BF16), a


def _s5_pre_kernel(x_ref, g_ref, o_ref):
    xn = _rmsnorm(x_ref[0], g_ref[...])
    for c in range(TOKEN_TILE // S5_CHUNK):
        o_ref[c, 0] = xn[c * S5_CHUNK:(c + 1) * S5_CHUNK, :].T.astype(BF16)


def _s5_pre(x3d, g):
    bsz, seq, _ = x3d.shape
    per_tile = TOKEN_TILE // S5_CHUNK
    return pl.pallas_call(
        _s5_pre_kernel,
        grid=(bsz, seq // TOKEN_TILE),
        in_specs=[pl.BlockSpec((1, TOKEN_TILE, D_MODEL), lambda b, i: (b, i, 0)),
                  _resident((1, D_MODEL))],
        out_specs=pl.BlockSpec((per_tile, 1, D_MODEL, S5_CHUNK), lambda b, i: (i, b, 0, 0)),
        out_shape=jax.ShapeDtypeStruct((seq // S5_CHUNK, bsz, D_MODEL, S5_CHUNK), BF16),
        compiler_params=_params(2),
        name="s5_pre",
    )(x3d, g)


def _s5_core_kernel(u_ref, t_ref, w_ref, z_ref, a_ref, y_ref, v_ref, xin_ref, *, bsz, n_chunks):
    u = u_ref[...]
    v_ref[...] = _dot(u, w_ref[0])
    half = 2 * S5_STATE
    a_same = a_ref[0, 0:1, :]
    a_cross_p = a_ref[0, 1:2, :]
    a_cross_q = a_ref[0, 2:3, :]
    p = jnp.zeros((bsz, half), F32)
    q = jnp.zeros((bsz, half), F32)
    for c in range(n_chunks):
        rows = slice(c * bsz, (c + 1) * bsz)
        xin_ref[rows, :] = p
        vc = v_ref[rows, :]
        p, q = (a_same * p + a_cross_p * q + vc[:, :half],
                a_same * q + a_cross_q * p + vc[:, half:])
    y_ref[...] = _dot(u, t_ref[0]) + _dot(xin_ref[...].astype(BF16), z_ref[0])


def _s5_core(u2d, toep, w, z, a, bsz):
    rows = u2d.shape[0]
    return pl.pallas_call(
        functools.partial(_s5_core_kernel, bsz=bsz, n_chunks=rows // bsz),
        grid=(S5_GROUPS,),
        in_specs=[pl.BlockSpec((rows, S5_COLS), lambda g: (0, g)),
                  pl.BlockSpec((1, S5_COLS, S5_COLS), lambda g: (g, 0, 0)),
                  pl.BlockSpec((1, S5_COLS, 4 * S5_STATE), lambda g: (g, 0, 0)),
                  pl.BlockSpec((1, 2 * S5_STATE, S5_COLS), lambda g: (g, 0, 0)),
                  pl.BlockSpec((1, 8, 2 * S5_STATE), lambda g: (g, 0, 0))],
        out_specs=pl.BlockSpec((rows, S5_COLS), lambda g: (0, g)),
        out_shape=jax.ShapeDtypeStruct((rows, S5_GROUPS * S5_COLS), F32),
        scratch_shapes=[pltpu.VMEM((rows, 4 * S5_STATE), F32),
                        pltpu.VMEM((rows, 2 * S5_STATE), F32)],
        compiler_params=_params(1),
        name="s5_core",
    )(u2d, toep, w, z, a)


def _gelu_tanh(x):
    return 0.5 * x * (1.0 + jnp.tanh(math.sqrt(2.0 / math.pi) * (x + 0.044715 * (x * x * x))))


def _s5_post_kernel(x_ref, y_ref, g_ref, d_ref, w_ref, o_ref, act_ref):
    x = x_ref[0]
    skip = d_ref[...] * _rmsnorm(x, g_ref[...])
    for c in range(TOKEN_TILE // S5_CHUNK):
        rows = slice(c * S5_CHUNK, (c + 1) * S5_CHUNK)
        act_ref[rows, :] = _gelu_tanh(y_ref[c, 0].T + skip[rows, :]).astype(BF16)
    z = _dot(act_ref[...], w_ref[...])
    o_ref[0] = x + z[:, :D_MODEL] * jax.nn.sigmoid(z[:, D_MODEL:])


def _s5_post(x3d, y4d, g, d_skip, w_glu):
    bsz, seq, _ = x3d.shape
    per_tile = TOKEN_TILE // S5_CHUNK
    row = pl.BlockSpec((1, TOKEN_TILE, D_MODEL), lambda b, i: (b, i, 0))
    return pl.pallas_call(
        _s5_post_kernel,
        grid=(bsz, seq // TOKEN_TILE),
        in_specs=[row,
                  pl.BlockSpec((per_tile, 1, D_MODEL, S5_CHUNK), lambda b, i: (i, b, 0, 0)),
                  _resident((1, D_MODEL)), _resident((1, D_MODEL)),
                  _resident((D_MODEL, 2 * D_MODEL))],
        out_specs=row,
        out_shape=jax.ShapeDtypeStruct(x3d.shape, F32),
        scratch_shapes=[pltpu.VMEM((TOKEN_TILE, D_MODEL), BF16)],
        compiler_params=_params(2),
        name="s5_post",
    )(x3d, y4d, g, d_skip, w_glu)


def kernel(x, mem, positions, norm_gains, mem_norm, final_norm, ffn1_w_in, ffn1_w_out, ffn2_w_in, ffn2_w_out, s5_lam_re, s5_lam_im, s5_log_dt, s5_b_re, s5_b_im, s5_c_re, s5_c_im, s5_d, s5_w_glu, ret_w_in, ret_w_out, xattn_w_q, xattn_w_kv, xattn_w_o):
    bsz, seq, _ = x.shape
    tokens = bsz * seq
    n_chunks = seq // S5_CHUNK
    gains = norm_gains.astype(F32)
    gf = final_norm.astype(F32).reshape(1, D_MODEL)

    kv = _memory_kv(mem.reshape(bsz * N_MEM, D_MODEL), mem_norm.astype(F32).reshape(1, D_MODEL),
                    xattn_w_kv.astype(BF16))
    kv = kv.reshape(DEPTH, bsz, N_MEM, 2 * D_MODEL)
    pos_lanes = jnp.broadcast_to(positions.astype(F32).reshape(tokens, 1), (tokens, ROPE_HALF))
    inv_freq = (1.0 / (ROPE_BASE ** jnp.linspace(0.0, 1.0, ROPE_HALF, dtype=F32))).reshape(1, ROPE_HALF)

    for i in range(DEPTH):
        g = [gains[i, j].reshape(1, D_MODEL) for j in range(4)]
        x2d = _ffn(x.reshape(tokens, D_MODEL), g[0], ffn1_w_in[i].astype(BF16),
                   ffn1_w_out[i].astype(BF16), gf, False)
        j = i // 2
        if i % 2 == 0:
            toep, w, z, a = _s5_operators(s5_lam_re[j], s5_lam_im[j], s5_log_dt[j], s5_b_re[j],
                                          s5_b_im[j], s5_c_re[j], s5_c_im[j])
            x = x2d.reshape(bsz, seq, D_MODEL)
            u = _s5_pre(x, g[1]).reshape(n_chunks * bsz, S5_GROUPS * S5_COLS)
            y = _s5_core(u, toep, w, z, a, bsz).reshape(n_chunks, bsz, D_MODEL, S5_CHUNK)
            x = _s5_post(x, y, g[1], s5_d[j].astype(F32).reshape(1, D_MODEL),
                         s5_w_glu[j].astype(BF16))
        else:
            q, k, v, gate = _ret_in(x2d, g[1], pos_lanes, inv_freq, ret_w_in[j].astype(BF16))
            x = _ret_core(x2d.reshape(bsz, seq, D_MODEL), q.reshape(bsz, seq, D_MODEL),
                          k.reshape(bsz, seq, D_MODEL), v.reshape(bsz, seq, RET_VALUE_WIDTH),
                          gate.reshape(bsz, seq, RET_VALUE_WIDTH), ret_w_out[j].astype(BF16))
        x = _xattn(x, g[2], xattn_w_q[i].astype(BF16), kv[i], xattn_w_o[i].astype(BF16))
        x = _ffn(x.reshape(tokens, D_MODEL), g[3], ffn2_w_in[i].astype(BF16),
                 ffn2_w_out[i].astype(BF16), gf, i == DEPTH - 1).reshape(bsz, seq, D_MODEL)
    return x
```

```python
import functools
import math

import jax
import jax.numpy as jnp
from jax import lax
from jax.experimental import pallas as pl
from jax.experimental.pallas import tpu as pltpu

F32 = jnp.float32
BF16 = jnp.bfloat16

D_MODEL = 1024
DEPTH = 4
N_MEM = 256
D_FF = 2816
MACARON_WEIGHT = 0.5
EPS = 1e-6

S5_GROUP = 16
S5_GROUPS = D_MODEL // S5_GROUP
S5_STATE = 64
S5_CHUNK = 128
S5_COLS = S5_GROUP * S5_CHUNK
S5_PRE_CHUNKS = 16
S5_POST_CHUNKS = 8

RET_HEADS = 4
RET_QK_DIM = D_MODEL // RET_HEADS
RET_V_DIM = 2 * RET_QK_DIM
RET_VALUE_WIDTH = RET_HEADS * RET_V_DIM
RET_IN_WIDTH = 2 * D_MODEL + 2 * RET_VALUE_WIDTH
RET_CHUNK = 256
ROPE_BASE = 10000.0
ROPE_HALF = RET_QK_DIM // 2

XATTN_HEADS = 4
XATTN_DIM = D_MODEL // XATTN_HEADS

TOKEN_TILE = 512
VMEM_LIMIT = 56 * 1024 * 1024


def _params(n_axes):
    return pltpu.CompilerParams(dimension_semantics=("arbitrary",) * n_axes,
                                vmem_limit_bytes=VMEM_LIMIT)


def _resident(shape):
    zeros = (0,) * len(shape)
    return pl.BlockSpec(shape, lambda *_: zeros, pipeline_mode=pl.Buffered(1))


def _rmsnorm(x, g):
    return x * lax.rsqrt(jnp.mean(x * x, axis=-1, keepdims=True) + EPS) * g


def _dot(a, b):
    return jnp.dot(a, b, preferred_element_type=F32)


def _ffn_kernel(x_ref, g_ref, w_in_ref, w_out_ref, gf_ref, o_ref, *, final_norm):
    x = x_ref[...]
    xn = _rmsnorm(x, g_ref[...]).astype(BF16)
    h = _dot(xn, w_in_ref[...])
    a = h[:, :D_FF]
    act = (a * jax.nn.sigmoid(a) * h[:, D_FF:]).astype(BF16)
    y = x + MACARON_WEIGHT * _dot(act, w_out_ref[...])
    if final_norm:
        y = _rmsnorm(y, gf_ref[...])
    o_ref[...] = y


def _ffn(x2d, g, w_in, w_out, gf, final_norm):
    tokens = x2d.shape[0]
    row = pl.BlockSpec((TOKEN_TILE, D_MODEL), lambda i: (i, 0))
    return pl.pallas_call(
        functools.partial(_ffn_kernel, final_norm=final_norm),
        grid=(tokens // TOKEN_TILE,),
        in_specs=[row, _resident((1, D_MODEL)), _resident((D_MODEL, 2 * D_FF)),
                  _resident((D_FF, D_MODEL)), _resident((1, D_MODEL))],
        out_specs=row,
        out_shape=jax.ShapeDtypeStruct((tokens, D_MODEL), F32),
        compiler_params=_params(1),
        name="ffn",
    )(x2d, g, w_in, w_out, gf)


def _kv_kernel(mem_ref, g_ref, w_ref, o_ref):
    mn = _rmsnorm(mem_ref[...], g_ref[...]).astype(BF16)
    o_ref[0] = _dot(mn, w_ref[0]).astype(BF16)


def _memory_kv(mem2d, g, w_kv):
    rows = mem2d.shape[0]
    half = rows // 2
    return pl.pallas_call(
        _kv_kernel,
        grid=(DEPTH, 2),
        in_specs=[pl.BlockSpec((half, D_MODEL), lambda l, r: (r, 0)),
                  _resident((1, D_MODEL)),
                  pl.BlockSpec((1, D_MODEL, 2 * D_MODEL), lambda l, r: (l, 0, 0))],
        out_specs=pl.BlockSpec((1, half, 2 * D_MODEL), lambda l, r: (l, r, 0)),
        out_shape=jax.ShapeDtypeStruct((DEPTH, rows, 2 * D_MODEL), BF16),
        compiler_params=_params(2),
        name="memory_kv",
    )(mem2d, g, w_kv)


def _xattn_kernel(x_ref, g_ref, wq_ref, k_ref, v_ref, wo_ref, o_ref):
    x = x_ref[0]
    xn = _rmsnorm(x, g_ref[...]).astype(BF16)
    q = (_dot(xn, wq_ref[...]) * (XATTN_DIM ** -0.5)).astype(BF16)
    heads = []
    for h in range(XATTN_HEADS):
        sl = slice(h * XATTN_DIM, (h + 1) * XATTN_DIM)
        s = lax.dot_general(q[:, sl], k_ref[0, :, sl], (((1,), (1,)), ((), ())),
                            preferred_element_type=F32)
        e = jnp.exp(s - jnp.max(s, axis=-1, keepdims=True))
        p = e / jnp.sum(e, axis=-1, keepdims=True)
        heads.append(_dot(p.astype(BF16), v_ref[0, :, sl]).astype(BF16))
    o = jnp.concatenate(heads, axis=-1)
    o_ref[0] = x + _dot(o, wo_ref[...])


def _xattn(x3d, g, w_q, kv, w_o):
    bsz, seq, _ = x3d.shape
    row = pl.BlockSpec((1, TOKEN_TILE, D_MODEL), lambda b, i: (b, i, 0))
    return pl.pallas_call(
        _xattn_kernel,
        grid=(bsz, seq // TOKEN_TILE),
        in_specs=[row, _resident((1, D_MODEL)), _resident((D_MODEL, D_MODEL)),
                  pl.BlockSpec((1, N_MEM, D_MODEL), lambda b, i: (b, 0, 0)),
                  pl.BlockSpec((1, N_MEM, D_MODEL), lambda b, i: (b, 0, 1)),
                  _resident((D_MODEL, D_MODEL))],
        out_specs=row,
        out_shape=jax.ShapeDtypeStruct(x3d.shape, F32),
        compiler_params=_params(2),
        name="xattn",
    )(x3d, g, w_q, kv, kv, w_o)


def _ret_in_kernel(x_ref, g_ref, pos_ref, freq_ref, w_ref, q_ref, k_ref, v_ref, gate_ref):
    xn = _rmsnorm(x_ref[...], g_ref[...]).astype(BF16)
    h = _dot(xn, w_ref[...])
    ang = pos_ref[...] * freq_ref[...]
    cos = jnp.cos(ang)
    sin = jnp.sin(ang)
    k_scale = RET_QK_DIM ** -0.5
    for hd in range(RET_HEADS):
        lo = hd * RET_QK_DIM
        mid = lo + ROPE_HALF
        hi = lo + RET_QK_DIM
        q1, q2 = h[:, lo:mid], h[:, mid:hi]
        q_ref[:, lo:mid] = (q1 * cos - q2 * sin).astype(BF16)
        q_ref[:, mid:hi] = (q1 * sin + q2 * cos).astype(BF16)
        k1 = h[:, D_MODEL + lo:D_MODEL + mid] * k_scale
        k2 = h[:, D_MODEL + mid:D_MODEL + hi] * k_scale
        k_ref[:, lo:mid] = (k1 * cos - k2 * sin).astype(BF16)
        k_ref[:, mid:hi] = (k1 * sin + k2 * cos).astype(BF16)
    v_ref[...] = h[:, 2 * D_MODEL:2 * D_MODEL + RET_VALUE_WIDTH].astype(BF16)
    gate_ref[...] = h[:, 2 * D_MODEL + RET_VALUE_WIDTH:].astype(BF16)


def _ret_in(x2d, g, pos_lanes, inv_freq, w_in):
    tokens = x2d.shape[0]

    def row(width):
        return pl.BlockSpec((TOKEN_TILE, width), lambda i: (i, 0))

    def out(width):
        return jax.ShapeDtypeStruct((tokens, width), BF16)

    return pl.pallas_call(
        _ret_in_kernel,
        grid=(tokens // TOKEN_TILE,),
        in_specs=[row(D_MODEL), _resident((1, D_MODEL)), row(ROPE_HALF),
                  _resident((1, ROPE_HALF)), _resident((D_MODEL, RET_IN_WIDTH))],
        out_specs=[row(D_MODEL), row(D_MODEL), row(RET_VALUE_WIDTH), row(RET_VALUE_WIDTH)],
        out_shape=[out(D_MODEL), out(D_MODEL), out(RET_VALUE_WIDTH), out(RET_VALUE_WIDTH)],
        compiler_params=_params(1),
        name="ret_in",
    )(x2d, g, pos_lanes, inv_freq, w_in)


def _ret_core_kernel(x_ref, q_ref, k_ref, v_ref, gate_ref, w_out_ref, o_ref, state_ref, y_ref):
    @pl.when(pl.program_id(1) == 0)
    def _():
        state_ref[...] = jnp.zeros_like(state_ref)

    c = RET_CHUNK
    row_qk = lax.broadcasted_iota(jnp.int32, (c, RET_QK_DIM), 0).astype(F32)
    row_v = lax.broadcasted_iota(jnp.int32, (c, RET_V_DIM), 0).astype(F32)
    diff = (lax.broadcasted_iota(jnp.int32, (c, c), 0)
            - lax.broadcasted_iota(jnp.int32, (c, c), 1))
    lag = jnp.maximum(diff, 0).astype(F32)
    for hd in range(RET_HEADS):
        log_gamma = math.log(1.0 - 2.0 ** (-5 - hd))
        d_mat = jnp.where(diff >= 0, jnp.exp(log_gamma * lag), 0.0)
        xi = jnp.exp(log_gamma * (row_v + 1.0))
        zeta = jnp.exp(log_gamma * (c - 1.0 - row_qk))
        gamma_chunk = math.exp(log_gamma * c)
        qk = slice(hd * RET_QK_DIM, (hd + 1) * RET_QK_DIM)
        vv = slice(hd * RET_V_DIM, (hd + 1) * RET_V_DIM)
        for ci in range(TOKEN_TILE // c):
            rows = slice(ci * c, (ci + 1) * c)
            qc = q_ref[0, rows, qk]
            kc = k_ref[0, rows, qk]
            vc = v_ref[0, rows, vv]
            state = state_ref[hd]
            inner = lax.dot_general(qc, kc, (((1,), (1,)), ((), ())),
                                    preferred_element_type=F32) * d_mat
            o = _dot(inner.astype(BF16), vc) + _dot(qc, state.astype(BF16)) * xi
            kz = (kc.astype(F32) * zeta).astype(BF16)
            state_ref[hd] = state * gamma_chunk + lax.dot_general(
                kz, vc, (((0,), (0,)), ((), ())), preferred_element_type=F32)
            mu = jnp.mean(o, axis=-1, keepdims=True)
            dev = o - mu
            var = jnp.mean(dev * dev, axis=-1, keepdims=True)
            gate = gate_ref[0, rows, vv].astype(F32)
            y = gate * jax.nn.sigmoid(gate) * (dev * lax.rsqrt(var + EPS))
            y_ref[rows, vv] = y.astype(BF16)
    o_ref[0] = x_ref[0] + _dot(y_ref[...], w_out_ref[...])


def _ret_core(x3d, q, k, v, gate, w_out):
    bsz, seq, _ = x3d.shape

    def row(width):
        return pl.BlockSpec((1, TOKEN_TILE, width), lambda b, i: (b, i, 0))

    return pl.pallas_call(
        _ret_core_kernel,
        grid=(bsz, seq // TOKEN_TILE),
        in_specs=[row(D_MODEL), row(D_MODEL), row(D_MODEL), row(RET_VALUE_WIDTH),
                  row(RET_VALUE_WIDTH), _resident((RET_VALUE_WIDTH, D_MODEL))],
        out_specs=row(D_MODEL),
        out_shape=jax.ShapeDtypeStruct(x3d.shape, F32),
        scratch_shapes=[pltpu.VMEM((RET_HEADS, RET_QK_DIM, RET_V_DIM), F32),
                        pltpu.VMEM((TOKEN_TILE, RET_VALUE_WIDTH), BF16)],
        compiler_params=_params(2),
        name="ret_core",
    )(x3d, q, k, v, gate, w_out)


def _s5_operators(lam_re, lam_im, log_dt, b_re, b_im, c_re, c_im):
    n = S5_CHUNK
    hp = lax.Precision.HIGHEST
    lr, li = lam_re.astype(F32), lam_im.astype(F32)
    dt = jnp.exp(log_dt.astype(F32))[:, None]
    mag = jnp.exp(lr * dt)
    ar = mag * jnp.cos(li * dt)
    ai = mag * jnp.sin(li * dt)
    den = lr * lr + li * li
    nr = ar - 1.0
    zr = (nr * lr + ai * li) / den
    zi = (ai * lr - nr * li) / den
    bbr = zr[..., None] * b_re - zi[..., None] * b_im
    bbi = zr[..., None] * b_im + zi[..., None] * b_re
    steps = jnp.arange(n + 1, dtype=F32)[None, :, None]
    pmag = jnp.exp((lr * dt)[:, None, :] * steps)
    phase = (li * dt)[:, None, :] * steps
    pr = pmag * jnp.cos(phase)
    pi = pmag * jnp.sin(phase)
    clr = c_re[:, :, None, :] * pr[:, None] - c_im[:, :, None, :] * pi[:, None]
    cli = c_re[:, :, None, :] * pi[:, None] + c_im[:, :, None, :] * pr[:, None]
    resp = (jnp.einsum('gotp,gph->ghot', clr[:, :, :n], bbr, precision=hp)
            - jnp.einsum('gotp,gph->ghot', cli[:, :, :n], bbi, precision=hp))
    resp = resp.reshape(S5_GROUPS, S5_GROUP * S5_GROUP, n)
    prr = pr[:, n - 1::-1][:, None]
    pir = pi[:, n - 1::-1][:, None]
    bbr_t = bbr.transpose(0, 2, 1)[:, :, None, :]
    bbi_t = bbi.transpose(0, 2, 1)[:, :, None, :]
    wr = prr * bbr_t - pir * bbi_t
    wi = prr * bbi_t + pir * bbr_t
    w = jnp.concatenate([wr, wi, wi, wr], axis=-1).reshape(S5_GROUPS, S5_COLS, 4 * S5_STATE)
    zr_ = clr[:, :, 1:].transpose(0, 3, 1, 2)
    zi_ = -cli[:, :, 1:].transpose(0, 3, 1, 2)
    z = jnp.concatenate([zr_, zi_], axis=1).reshape(S5_GROUPS, 2 * S5_STATE, S5_COLS)
    alr, ali = pr[:, n], pi[:, n]
    zero = jnp.zeros_like(alr)
    a = jnp.stack([jnp.concatenate([alr, alr], -1), jnp.concatenate([-ali, ali], -1),
                   jnp.concatenate([ali, -ali], -1)] + [jnp.concatenate([zero, zero], -1)] * 5,
                  axis=1)
    return resp, w.astype(BF16), z.astype(BF16), a


def _s5_pre_kernel(x_ref, g_ref, o_ref, t_ref):
    for c in range(S5_PRE_CHUNKS):
        xn = _rmsnorm(x_ref[0, c * S5_CHUNK:(c + 1) * S5_CHUNK, :], g_ref[...])
        t_ref[c * D_MODEL:(c + 1) * D_MODEL, :] = xn.T

    def gather_channel(ch, carry):
        tile = t_ref[pl.ds(ch, S5_PRE_CHUNKS, stride=D_MODEL), :]
        o_ref[:, pl.ds(pl.multiple_of(ch * S5_CHUNK, S5_CHUNK), S5_CHUNK)] = tile.astype(BF16)
        return carry

    lax.fori_loop(0, D_MODEL, gather_channel, 0, unroll=8)


def _s5_pre(x3d, g):
    bsz, seq, _ = x3d.shape
    tile = S5_PRE_CHUNKS * S5_CHUNK
    steps = seq // tile
    return pl.pallas_call(
        _s5_pre_kernel,
        grid=(bsz, steps),
        in_specs=[pl.BlockSpec((1, tile, D_MODEL), lambda b, i: (b, i, 0)),
                  _resident((1, D_MODEL))],
        out_specs=pl.BlockSpec((S5_PRE_CHUNKS, D_MODEL * S5_CHUNK), lambda b, i: (b * steps + i, 0)),
        out_shape=jax.ShapeDtypeStruct((bsz * seq // S5_CHUNK, D_MODEL * S5_CHUNK), BF16),
        scratch_shapes=[pltpu.VMEM((S5_PRE_CHUNKS * D_MODEL, S5_CHUNK), F32)],
        compiler_params=_params(2),
        name="s5_pre",
    )(x3d, g)


def _s5_core_kernel(u_ref, r_ref, w_ref, z_ref, a_ref, y_ref, t_ref, v_ref, xin_ref, *, bsz, n_chunks):
    n = S5_CHUNK
    causal = (lax.broadcasted_iota(jnp.int32, (n, n), 1)
              >= lax.broadcasted_iota(jnp.int32, (n, n), 0))

    def build_rows(hi, carry):
        for ho in range(S5_GROUP):
            lags = jnp.broadcast_to(r_ref[0, pl.ds(hi * S5_GROUP + ho, 1), :], (n, n))
            blk = pltpu.roll(lags, 0, 1, stride=1, stride_axis=0)
            t_ref[pl.ds(pl.multiple_of(hi * n, n), n), ho * n:(ho + 1) * n] = (
                jnp.where(causal, blk, 0.0).astype(BF16))
        return carry

    lax.fori_loop(0, S5_GROUP, build_rows, 0)

    u = u_ref[...]
    half = 2 * S5_STATE
    v = _dot(u, w_ref[0])
    v_ref[0] = v[:, :half]
    v_ref[1] = v[:, half:]
    a_same = a_ref[0, 0:1, :]
    a_cross_p = a_ref[0, 1:2, :]
    a_cross_q = a_ref[0, 2:3, :]
    p = jnp.zeros((bsz, half), F32)
    q = jnp.zeros((bsz, half), F32)
    for c in range(n_chunks):
        xin_ref[pl.ds(c, bsz, stride=n_chunks), :] = p
        p, q = (a_same * p + a_cross_p * q + v_ref[0, pl.ds(c, bsz, stride=n_chunks), :],
                a_same * q + a_cross_q * p + v_ref[1, pl.ds(c, bsz, stride=n_chunks), :])
    y_ref[...] = _dot(u, t_ref[...]) + _dot(xin_ref[...].astype(BF16), z_ref[0])


def _s5_core(u2d, resp, w, z, a, bsz):
    rows = u2d.shape[0]
    return pl.pallas_call(
        functools.partial(_s5_core_kernel, bsz=bsz, n_chunks=rows // bsz),
        grid=(S5_GROUPS,),
        in_specs=[pl.BlockSpec((rows, S5_COLS), lambda g: (0, g)),
                  pl.BlockSpec((1, S5_GROUP * S5_GROUP, S5_CHUNK), lambda g: (g, 0, 0)),
                  pl.BlockSpec((1, S5_COLS, 4 * S5_STATE), lambda g: (g, 0, 0)),
                  pl.BlockSpec((1, 2 * S5_STATE, S5_COLS), lambda g: (g, 0, 0)),
                  pl.BlockSpec((1, 8, 2 * S5_STATE), lambda g: (g, 0, 0))],
        out_specs=pl.BlockSpec((rows, S5_COLS), lambda g: (0, g)),
        out_shape=jax.ShapeDtypeStruct((rows, S5_GROUPS * S5_COLS), F32),
        scratch_shapes=[pltpu.VMEM((S5_COLS, S5_COLS), BF16),
                        pltpu.VMEM((2, rows, 2 * S5_STATE), F32),
                        pltpu.VMEM((rows, 2 * S5_STATE), F32)],
        compiler_params=_params(1),
        name="s5_core",
    )(u2d, resp, w, z, a)


def _gelu_tanh(x):
    return 0.5 * x * (1.0 + jnp.tanh(math.sqrt(2.0 / math.pi) * (x + 0.044715 * (x * x * x))))


def _s5_post_kernel(x_ref, y_ref, g_ref, d_ref, w_ref, o_ref, t_ref, act_ref):
    def scatter_channel(ch, carry):
        t_ref[pl.ds(ch, S5_POST_CHUNKS, stride=D_MODEL), :] = (
            y_ref[:, pl.ds(pl.multiple_of(ch * S5_CHUNK, S5_CHUNK), S5_CHUNK)])
        return carry

    lax.fori_loop(0, D_MODEL, scatter_channel, 0, unroll=8)

    per_tile = TOKEN_TILE // S5_CHUNK
    for part in range(S5_POST_CHUNKS // per_tile):
        tok = slice(part * TOKEN_TILE, (part + 1) * TOKEN_TILE)
        x = x_ref[0, tok, :]
        skip = d_ref[...] * _rmsnorm(x, g_ref[...])
        for c in range(per_tile):
            rows = slice(c * S5_CHUNK, (c + 1) * S5_CHUNK)
            chunk = part * per_tile + c
            y = t_ref[chunk * D_MODEL:(chunk + 1) * D_MODEL, :].T
            act_ref[rows, :] = _gelu_tanh(y + skip[rows, :]).astype(BF16)
        z = _dot(act_ref[...], w_ref[...])
        o_ref[0, tok, :] = x + z[:, :D_MODEL] * jax.nn.sigmoid(z[:, D_MODEL:])


def _s5_post(x3d, y2d, g, d_skip, w_glu):
    bsz, seq, _ = x3d.shape
    tile = S5_POST_CHUNKS * S5_CHUNK
    steps = seq // tile
    row = pl.BlockSpec((1, tile, D_MODEL), lambda b, i: (b, i, 0))
    return pl.pallas_call(
        _s5_post_kernel,
        grid=(bsz, steps),
        in_specs=[row,
                  pl.BlockSpec((S5_POST_CHUNKS, D_MODEL * S5_CHUNK), lambda b, i: (b * steps + i, 0)),
                  _resident((1, D_MODEL)), _resident((1, D_MODEL)),
                  _resident((D_MODEL, 2 * D_MODEL))],
        out_specs=row,
        out_shape=jax.ShapeDtypeStruct(x3d.shape, F32),
        scratch_shapes=[pltpu.VMEM((S5_POST_CHUNKS * D_MODEL, S5_CHUNK), F32),
                        pltpu.VMEM((TOKEN_TILE, D_MODEL), BF16)],
        compiler_params=_params(2),
        name="s5_post",
    )(x3d, y2d, g, d_skip, w_glu)


def kernel(x, mem, positions, norm_gains, mem_norm, final_norm, ffn1_w_in, ffn1_w_out, ffn2_w_in, ffn2_w_out, s5_lam_re, s5_lam_im, s5_log_dt, s5_b_re, s5_b_im, s5_c_re, s5_c_im, s5_d, s5_w_glu, ret_w_in, ret_w_out, xattn_w_q, xattn_w_kv, xattn_w_o):
    bsz, seq, _ = x.shape
    tokens = bsz * seq
    gains = norm_gains.astype(F32)
    gf = final_norm.astype(F32).reshape(1, D_MODEL)

    kv = _memory_kv(mem.reshape(bsz * N_MEM, D_MODEL), mem_norm.astype(F32).reshape(1, D_MODEL),
                    xattn_w_kv.astype(BF16))
    kv = kv.reshape(DEPTH, bsz, N_MEM, 2 * D_MODEL)
    pos_lanes = jnp.broadcast_to(positions.astype(F32).reshape(tokens, 1), (tokens, ROPE_HALF))
    inv_freq = (1.0 / (ROPE_BASE ** jnp.linspace(0.0, 1.0, ROPE_HALF, dtype=F32))).reshape(1, ROPE_HALF)

    for i in range(DEPTH):
        g = [gains[i, j].reshape(1, D_MODEL) for j in range(4)]
        x2d = _ffn(x.reshape(tokens, D_MODEL), g[0], ffn1_w_in[i].astype(BF16),
                   ffn1_w_out[i].astype(BF16), gf, False)
        j = i // 2
        if i % 2 == 0:
            resp, w, z, a = _s5_operators(s5_lam_re[j], s5_lam_im[j], s5_log_dt[j], s5_b_re[j],
                                          s5_b_im[j], s5_c_re[j], s5_c_im[j])
            x = x2d.reshape(bsz, seq, D_MODEL)
            y = _s5_core(_s5_pre(x, g[1]), resp, w, z, a, bsz)
            x = _s5_post(x, y, g[1], s5_d[j].astype(F32).reshape(1, D_MODEL),
                         s5_w_glu[j].astype(BF16))
        else:
            q, k, v, gate = _ret_in(x2d, g[1], pos_lanes, inv_freq, ret_w_in[j].astype(BF16))
            x = _ret_core(x2d.reshape(bsz, seq, D_MODEL), q.reshape(bsz, seq, D_MODEL),
                          k.reshape(bsz, seq, D_MODEL), v.reshape(bsz, seq, RET_VALUE_WIDTH),
                          gate.reshape(bsz, seq, RET_VALUE_WIDTH), ret_w_out[j].astype(BF16))
        x = _xattn(x, g[2], xattn_w_q[i].astype(BF16), kv[i], xattn_w_o[i].astype(BF16))
        x = _ffn(x.reshape(tokens, D_MODEL), g[3], ffn2_w_in[i].astype(BF16),
                 ffn2_w_out[i].astype(BF16), gf, i == DEPTH - 1).reshape(bsz, seq, D_MODEL)
    return x
```

```python
import functools
import math

import jax
import jax.numpy as jnp
from jax import lax
from jax.experimental import pallas as pl
from jax.experimental.pallas import tpu as pltpu

F32 = jnp.float32
BF16 = jnp.bfloat16

D_MODEL = 1024
DEPTH = 4
N_MEM = 256
D_FF = 2816
MACARON_WEIGHT = 0.5
EPS = 1e-6

S5_GROUP = 16
S5_GROUPS = D_MODEL // S5_GROUP
S5_STATE = 64
S5_CHUNK = 128
S5_COLS = S5_GROUP * S5_CHUNK
S5_TILE_CHUNKS = 4

RET_HEADS = 4
RET_QK_DIM = D_MODEL // RET_HEADS
RET_V_DIM = 2 * RET_QK_DIM
RET_VALUE_WIDTH = RET_HEADS * RET_V_DIM
RET_IN_WIDTH = 2 * D_MODEL + 2 * RET_VALUE_WIDTH
RET_CHUNK = 256
ROPE_BASE = 10000.0
ROPE_HALF = RET_QK_DIM // 2

XATTN_HEADS = 4
XATTN_DIM = D_MODEL // XATTN_HEADS

TOKEN_TILE = 512
VMEM_LIMIT = 56 * 1024 * 1024


def _params(n_axes):
    return pltpu.CompilerParams(dimension_semantics=("arbitrary",) * n_axes,
                                vmem_limit_bytes=VMEM_LIMIT)


def _resident(shape):
    zeros = (0,) * len(shape)
    return pl.BlockSpec(shape, lambda *_: zeros, pipeline_mode=pl.Buffered(1))


def _rmsnorm(x, g):
    return x * lax.rsqrt(jnp.mean(x * x, axis=-1, keepdims=True) + EPS) * g


def _dot(a, b):
    return jnp.dot(a, b, preferred_element_type=F32)


def _ffn_kernel(x_ref, g_ref, w_in_ref, w_out_ref, gf_ref, o_ref, *, final_norm):
    x = x_ref[...]
    xn = _rmsnorm(x, g_ref[...]).astype(BF16)
    h = _dot(xn, w_in_ref[...])
    a = h[:, :D_FF]
    act = (a * jax.nn.sigmoid(a) * h[:, D_FF:]).astype(BF16)
    y = x + MACARON_WEIGHT * _dot(act, w_out_ref[...])
    if final_norm:
        y = _rmsnorm(y, gf_ref[...])
    o_ref[...] = y


def _ffn(x2d, g, w_in, w_out, gf, final_norm):
    tokens = x2d.shape[0]
    row = pl.BlockSpec((TOKEN_TILE, D_MODEL), lambda i: (i, 0))
    return pl.pallas_call(
        functools.partial(_ffn_kernel, final_norm=final_norm),
        grid=(tokens // TOKEN_TILE,),
        in_specs=[row, _resident((1, D_MODEL)), _resident((D_MODEL, 2 * D_FF)),
                  _resident((D_FF, D_MODEL)), _resident((1, D_MODEL))],
        out_specs=row,
        out_shape=jax.ShapeDtypeStruct((tokens, D_MODEL), F32),
        compiler_params=_params(1),
        name="ffn",
    )(x2d, g, w_in, w_out, gf)


def _kv_kernel(mem_ref, g_ref, w_ref, o_ref):
    mn = _rmsnorm(mem_ref[...], g_ref[...]).astype(BF16)
    o_ref[0] = _dot(mn, w_ref[0]).astype(BF16)


def _memory_kv(mem2d, g, w_kv):
    rows = mem2d.shape[0]
    half = rows // 2
    return pl.pallas_call(
        _kv_kernel,
        grid=(DEPTH, 2),
        in_specs=[pl.BlockSpec((half, D_MODEL), lambda l, r: (r, 0)),
                  _resident((1, D_MODEL)),
                  pl.BlockSpec((1, D_MODEL, 2 * D_MODEL), lambda l, r: (l, 0, 0))],
        out_specs=pl.BlockSpec((1, half, 2 * D_MODEL), lambda l, r: (l, r, 0)),
        out_shape=jax.ShapeDtypeStruct((DEPTH, rows, 2 * D_MODEL), BF16),
        compiler_params=_params(2),
        name="memory_kv",
    )(mem2d, g, w_kv)


def _xattn_kernel(x_ref, g_ref, wq_ref, k_ref, v_ref, wo_ref, o_ref):
    x = x_ref[0]
    xn = _rmsnorm(x, g_ref[...]).astype(BF16)
    q = (_dot(xn, wq_ref[...]) * (XATTN_DIM ** -0.5)).astype(BF16)
    heads = []
    for h in range(XATTN_HEADS):
        sl = slice(h * XATTN_DIM, (h + 1) * XATTN_DIM)
        s = lax.dot_general(q[:, sl], k_ref[0, :, sl], (((1,), (1,)), ((), ())),
                            preferred_element_type=F32)
        e = jnp.exp(s - jnp.max(s, axis=-1, keepdims=True))
        p = e / jnp.sum(e, axis=-1, keepdims=True)
        heads.append(_dot(p.astype(BF16), v_ref[0, :, sl]).astype(BF16))
    o = jnp.concatenate(heads, axis=-1)
    o_ref[0] = x + _dot(o, wo_ref[...])


def _xattn(x3d, g, w_q, kv, w_o):
    bsz, seq, _ = x3d.shape
    row = pl.BlockSpec((1, TOKEN_TILE, D_MODEL), lambda b, i: (b, i, 0))
    return pl.pallas_call(
        _xattn_kernel,
        grid=(bsz, seq // TOKEN_TILE),
        in_specs=[row, _resident((1, D_MODEL)), _resident((D_MODEL, D_MODEL)),
                  pl.BlockSpec((1, N_MEM, D_MODEL), lambda b, i: (b, 0, 0)),
                  pl.BlockSpec((1, N_MEM, D_MODEL), lambda b, i: (b, 0, 1)),
                  _resident((D_MODEL, D_MODEL))],
        out_specs=row,
        out_shape=jax.ShapeDtypeStruct(x3d.shape, F32),
        compiler_params=_params(2),
        name="xattn",
    )(x3d, g, w_q, kv, kv, w_o)


def _ret_in_kernel(x_ref, g_ref, pos_ref, freq_ref, w_ref, q_ref, k_ref, v_ref, gate_ref):
    xn = _rmsnorm(x_ref[...], g_ref[...]).astype(BF16)
    h = _dot(xn, w_ref[...])
    ang = pos_ref[...] * freq_ref[...]
    cos = jnp.cos(ang)
    sin = jnp.sin(ang)
    k_scale = RET_QK_DIM ** -0.5
    for hd in range(RET_HEADS):
        lo = hd * RET_QK_DIM
        mid = lo + ROPE_HALF
        hi = lo + RET_QK_DIM
        q1, q2 = h[:, lo:mid], h[:, mid:hi]
        q_ref[:, lo:mid] = (q1 * cos - q2 * sin).astype(BF16)
        q_ref[:, mid:hi] = (q1 * sin + q2 * cos).astype(BF16)
        k1 = h[:, D_MODEL + lo:D_MODEL + mid] * k_scale
        k2 = h[:, D_MODEL + mid:D_MODEL + hi] * k_scale
        k_ref[:, lo:mid] = (k1 * cos - k2 * sin).astype(BF16)
        k_ref[:, mid:hi] = (k1 * sin + k2 * cos).astype(BF16)
    v_ref[...] = h[:, 2 * D_MODEL:2 * D_MODEL + RET_VALUE_WIDTH].astype(BF16)
    gate_ref[...] = h[:, 2 * D_MODEL + RET_VALUE_WIDTH:].astype(BF16)


def _ret_in(x2d, g, pos_lanes, inv_freq, w_in):
    tokens = x2d.shape[0]

    def row(width):
        return pl.BlockSpec((TOKEN_TILE, width), lambda i: (i, 0))

    def out(width):
        return jax.ShapeDtypeStruct((tokens, width), BF16)

    return pl.pallas_call(
        _ret_in_kernel,
        grid=(tokens // TOKEN_TILE,),
        in_specs=[row(D_MODEL), _resident((1, D_MODEL)), row(ROPE_HALF),
                  _resident((1, ROPE_HALF)), _resident((D_MODEL, RET_IN_WIDTH))],
        out_specs=[row(D_MODEL), row(D_MODEL), row(RET_VALUE_WIDTH), row(RET_VALUE_WIDTH)],
        out_shape=[out(D_MODEL), out(D_MODEL), out(RET_VALUE_WIDTH), out(RET_VALUE_WIDTH)],
        compiler_params=_params(1),
        name="ret_in",
    )(x2d, g, pos_lanes, inv_freq, w_in)


def _ret_core_kernel(x_ref, q_ref, k_ref, v_ref, gate_ref, w_out_ref, o_ref, state_ref, y_ref):
    @pl.when(pl.program_id(1) == 0)
    def _():
        state_ref[...] = jnp.zeros_like(state_ref)

    c = RET_CHUNK
    row_qk = lax.broadcasted_iota(jnp.int32, (c, RET_QK_DIM), 0).astype(F32)
    row_v = lax.broadcasted_iota(jnp.int32, (c, RET_V_DIM), 0).astype(F32)
    diff = (lax.broadcasted_iota(jnp.int32, (c, c), 0)
            - lax.broadcasted_iota(jnp.int32, (c, c), 1))
    lag = jnp.maximum(diff, 0).astype(F32)
    for hd in range(RET_HEADS):
        log_gamma = math.log(1.0 - 2.0 ** (-5 - hd))
        d_mat = jnp.where(diff >= 0, jnp.exp(log_gamma * lag), 0.0)
        xi = jnp.exp(log_gamma * (row_v + 1.0))
        zeta = jnp.exp(log_gamma * (c - 1.0 - row_qk))
        gamma_chunk = math.exp(log_gamma * c)
        qk = slice(hd * RET_QK_DIM, (hd + 1) * RET_QK_DIM)
        vv = slice(hd * RET_V_DIM, (hd + 1) * RET_V_DIM)
        for ci in range(TOKEN_TILE // c):
            rows = slice(ci * c, (ci + 1) * c)
            qc = q_ref[0, rows, qk]
            kc = k_ref[0, rows, qk]
            vc = v_ref[0, rows, vv]
            state = state_ref[hd]
            inner = lax.dot_general(qc, kc, (((1,), (1,)), ((), ())),
                                    preferred_element_type=F32) * d_mat
            o = _dot(inner.astype(BF16), vc) + _dot(qc, state.astype(BF16)) * xi
            kz = (kc.astype(F32) * zeta).astype(BF16)
            state_ref[hd] = state * gamma_chunk + lax.dot_general(
                kz, vc, (((0,), (0,)), ((), ())), preferred_element_type=F32)
            mu = jnp.mean(o, axis=-1, keepdims=True)
            dev = o - mu
            var = jnp.mean(dev * dev, axis=-1, keepdims=True)
            gate = gate_ref[0, rows, vv].astype(F32)
            y = gate * jax.nn.sigmoid(gate) * (dev * lax.rsqrt(var + EPS))
            y_ref[rows, vv] = y.astype(BF16)
    o_ref[0] = x_ref[0] + _dot(y_ref[...], w_out_ref[...])


def _ret_core(x3d, q, k, v, gate, w_out):
    bsz, seq, _ = x3d.shape

    def row(width):
        return pl.BlockSpec((1, TOKEN_TILE, width), lambda b, i: (b, i, 0))

    return pl.pallas_call(
        _ret_core_kernel,
        grid=(bsz, seq // TOKEN_TILE),
        in_specs=[row(D_MODEL), row(D_MODEL), row(D_MODEL), row(RET_VALUE_WIDTH),
                  row(RET_VALUE_WIDTH), _resident((RET_VALUE_WIDTH, D_MODEL))],
        out_specs=row(D_MODEL),
        out_shape=jax.ShapeDtypeStruct(x3d.shape, F32),
        scratch_shapes=[pltpu.VMEM((RET_HEADS, RET_QK_DIM, RET_V_DIM), F32),
                        pltpu.VMEM((TOKEN_TILE, RET_VALUE_WIDTH), BF16)],
        compiler_params=_params(2),
        name="ret_core",
    )(x3d, q, k, v, gate, w_out)


def _s5_operators(lam_re, lam_im, log_dt, b_re, b_im, c_re, c_im):
    n = S5_CHUNK
    hp = lax.Precision.HIGHEST
    lr, li = lam_re.astype(F32), lam_im.astype(F32)
    dt = jnp.exp(log_dt.astype(F32))[:, None]
    mag = jnp.exp(lr * dt)
    ar = mag * jnp.cos(li * dt)
    ai = mag * jnp.sin(li * dt)
    den = lr * lr + li * li
    nr = ar - 1.0
    zr = (nr * lr + ai * li) / den
    zi = (ai * lr - nr * li) / den
    bbr = zr[..., None] * b_re - zi[..., None] * b_im
    bbi = zr[..., None] * b_im + zi[..., None] * b_re
    steps = jnp.arange(n + 1, dtype=F32)[None, :, None]
    pmag = jnp.exp((lr * dt)[:, None, :] * steps)
    phase = (li * dt)[:, None, :] * steps
    pr = pmag * jnp.cos(phase)
    pi = pmag * jnp.sin(phase)
    clr = c_re[:, :, None, :] * pr[:, None] - c_im[:, :, None, :] * pi[:, None]
    cli = c_re[:, :, None, :] * pi[:, None] + c_im[:, :, None, :] * pr[:, None]
    resp = (jnp.einsum('gotp,gph->ghot', clr[:, :, :n], bbr, precision=hp)
            - jnp.einsum('gotp,gph->ghot', cli[:, :, :n], bbi, precision=hp))
    resp = resp.reshape(S5_GROUPS, S5_GROUP * S5_GROUP, n)
    prr = pr[:, n - 1::-1][:, None]
    pir = pi[:, n - 1::-1][:, None]
    bbr_t = bbr.transpose(0, 2, 1)[:, :, None, :]
    bbi_t = bbi.transpose(0, 2, 1)[:, :, None, :]
    wr = prr * bbr_t - pir * bbi_t
    wi = prr * bbi_t + pir * bbr_t
    w = jnp.concatenate([wr, wi, wi, wr], axis=-1).reshape(S5_GROUPS, S5_COLS, 4 * S5_STATE)
    zr_ = clr[:, :, 1:].transpose(0, 3, 1, 2)
    zi_ = -cli[:, :, 1:].transpose(0, 3, 1, 2)
    z = jnp.concatenate([zr_, zi_], axis=1).reshape(S5_GROUPS, 2 * S5_STATE, S5_COLS)
    alr, ali = pr[:, n], pi[:, n]
    zero = jnp.zeros_like(alr)
    a = jnp.stack([jnp.concatenate([alr, alr], -1), jnp.concatenate([-ali, ali], -1),
                   jnp.concatenate([ali, -ali], -1)] + [jnp.concatenate([zero, zero], -1)] * 5,
                  axis=1)
    return resp, w.astype(BF16), z.astype(BF16), a


def _s5_pre_kernel(x_ref, g_ref, o_ref):
    for c in range(S5_TILE_CHUNKS):
        xn = _rmsnorm(x_ref[0, c * S5_CHUNK:(c + 1) * S5_CHUNK, :], g_ref[...])
        o_ref[c] = xn.T


def _s5_pre(x3d, g):
    bsz, seq, _ = x3d.shape
    tile = S5_TILE_CHUNKS * S5_CHUNK
    steps = seq // tile
    return pl.pallas_call(
        _s5_pre_kernel,
        grid=(bsz, steps),
        in_specs=[pl.BlockSpec((1, tile, D_MODEL), lambda b, i: (b, i, 0)),
                  _resident((1, D_MODEL))],
        out_specs=pl.BlockSpec((S5_TILE_CHUNKS, D_MODEL, S5_CHUNK), lambda b, i: (b * steps + i, 0, 0)),
        out_shape=jax.ShapeDtypeStruct((bsz * seq // S5_CHUNK, D_MODEL, S5_CHUNK), F32),
        compiler_params=_params(2),
        name="s5_pre",
    )(x3d, g)


def _channel_rows(ref, ch):
    rows, channels, width = ref.shape
    return ref.reshape(rows * channels, width).at[pl.ds(ch, rows, stride=channels), :]


def _s5_core_kernel(u_ref, r_ref, w_ref, z_ref, a_ref, y_ref, t_ref, us_ref, v_ref, xin_ref, *,
                    bsz, n_chunks):
    n = S5_CHUNK
    causal = (lax.broadcasted_iota(jnp.int32, (n, n), 1)
              >= lax.broadcasted_iota(jnp.int32, (n, n), 0))

    def build_rows(hi, carry):
        for ho in range(S5_GROUP):
            lags = jnp.broadcast_to(r_ref[0, pl.ds(hi * S5_GROUP + ho, 1), :], (n, n))
            blk = pltpu.roll(lags, 0, 1, stride=1, stride_axis=0)
            t_ref[pl.ds(pl.multiple_of(hi * n, n), n), ho * n:(ho + 1) * n] = (
                jnp.where(causal, blk, 0.0).astype(BF16))
        return carry

    lax.fori_loop(0, S5_GROUP, build_rows, 0)

    for hi in range(S5_GROUP):
        us_ref[:, hi * n:(hi + 1) * n] = _channel_rows(u_ref, hi)[...].astype(BF16)
    u = us_ref[...]
    half = 2 * S5_STATE
    v = _dot(u, w_ref[0])
    v_ref[0] = v[:, :half]
    v_ref[1] = v[:, half:]
    a_same = a_ref[0, 0:1, :]
    a_cross_p = a_ref[0, 1:2, :]
    a_cross_q = a_ref[0, 2:3, :]
    p = jnp.zeros((bsz, half), F32)
    q = jnp.zeros((bsz, half), F32)
    for c in range(n_chunks):
        xin_ref[pl.ds(c, bsz, stride=n_chunks), :] = p
        p, q = (a_same * p + a_cross_p * q + v_ref[0, pl.ds(c, bsz, stride=n_chunks), :],
                a_same * q + a_cross_q * p + v_ref[1, pl.ds(c, bsz, stride=n_chunks), :])
    y = _dot(u, t_ref[...]) + _dot(xin_ref[...].astype(BF16), z_ref[0])
    for ho in range(S5_GROUP):
        _channel_rows(y_ref, ho)[...] = y[:, ho * n:(ho + 1) * n]


def _s5_core(u3d, resp, w, z, a, bsz):
    rows = u3d.shape[0]
    chunk_rows = pl.BlockSpec((rows, S5_GROUP, S5_CHUNK), lambda g: (0, g, 0))
    return pl.pallas_call(
        functools.partial(_s5_core_kernel, bsz=bsz, n_chunks=rows // bsz),
        grid=(S5_GROUPS,),
        in_specs=[chunk_rows,
                  pl.BlockSpec((1, S5_GROUP * S5_GROUP, S5_CHUNK), lambda g: (g, 0, 0)),
                  pl.BlockSpec((1, S5_COLS, 4 * S5_STATE), lambda g: (g, 0, 0)),
                  pl.BlockSpec((1, 2 * S5_STATE, S5_COLS), lambda g: (g, 0, 0)),
                  pl.BlockSpec((1, 8, 2 * S5_STATE), lambda g: (g, 0, 0))],
        out_specs=chunk_rows,
        out_shape=jax.ShapeDtypeStruct((rows, D_MODEL, S5_CHUNK), F32),
        scratch_shapes=[pltpu.VMEM((S5_COLS, S5_COLS), BF16),
                        pltpu.VMEM((rows, S5_COLS), BF16),
                        pltpu.VMEM((2, rows, 2 * S5_STATE), F32),
                        pltpu.VMEM((rows, 2 * S5_STATE), F32)],
        compiler_params=_params(1),
        name="s5_core",
    )(u3d, resp, w, z, a)


def _gelu_tanh(x):
    return 0.5 * x * (1.0 + jnp.tanh(math.sqrt(2.0 / math.pi) * (x + 0.044715 * (x * x * x))))


def _s5_post_kernel(x_ref, y_ref, g_ref, d_ref, w_ref, o_ref, act_ref):
    x = x_ref[0]
    skip = d_ref[...] * _rmsnorm(x, g_ref[...])
    for c in range(S5_TILE_CHUNKS):
        rows = slice(c * S5_CHUNK, (c + 1) * S5_CHUNK)
        act_ref[rows, :] = _gelu_tanh(y_ref[c].T + skip[rows, :]).astype(BF16)
    z = _dot(act_ref[...], w_ref[...])
    o_ref[0] = x + z[:, :D_MODEL] * jax.nn.sigmoid(z[:, D_MODEL:])


def _s5_post(x3d, y3d, g, d_skip, w_glu):
    bsz, seq, _ = x3d.shape
    tile = S5_TILE_CHUNKS * S5_CHUNK
    steps = seq // tile
    row = pl.BlockSpec((1, tile, D_MODEL), lambda b, i: (b, i, 0))
    return pl.pallas_call(
        _s5_post_kernel,
        grid=(bsz, steps),
        in_specs=[row,
                  pl.BlockSpec((S5_TILE_CHUNKS, D_MODEL, S5_CHUNK), lambda b, i: (b * steps + i, 0, 0)),
                  _resident((1, D_MODEL)), _resident((1, D_MODEL)),
                  _resident((D_MODEL, 2 * D_MODEL))],
        out_specs=row,
        out_shape=jax.ShapeDtypeStruct(x3d.shape, F32),
        scratch_shapes=[pltpu.VMEM((tile, D_MODEL), BF16)],
        compiler_params=_params(2),
        name="s5_post",
    )(x3d, y3d, g, d_skip, w_glu)


def kernel(x, mem, positions, norm_gains, mem_norm, final_norm, ffn1_w_in, ffn1_w_out, ffn2_w_in, ffn2_w_out, s5_lam_re, s5_lam_im, s5_log_dt, s5_b_re, s5_b_im, s5_c_re, s5_c_im, s5_d, s5_w_glu, ret_w_in, ret_w_out, xattn_w_q, xattn_w_kv, xattn_w_o):
    bsz, seq, _ = x.shape
    tokens = bsz * seq
    gains = norm_gains.astype(F32)
    gf = final_norm.astype(F32).reshape(1, D_MODEL)

    kv = _memory_kv(mem.reshape(bsz * N_MEM, D_MODEL), mem_norm.astype(F32).reshape(1, D_MODEL),
                    xattn_w_kv.astype(BF16))
    kv = kv.reshape(DEPTH, bsz, N_MEM, 2 * D_MODEL)
    pos_lanes = jnp.broadcast_to(positions.astype(F32).reshape(tokens, 1), (tokens, ROPE_HALF))
    inv_freq = (1.0 / (ROPE_BASE ** jnp.linspace(0.0, 1.0, ROPE_HALF, dtype=F32))).reshape(1, ROPE_HALF)

    for i in range(DEPTH):
        g = [gains[i, j].reshape(1, D_MODEL) for j in range(4)]
        x2d = _ffn(x.reshape(tokens, D_MODEL), g[0], ffn1_w_in[i].astype(BF16),
                   ffn1_w_out[i].astype(BF16), gf, False)
        j = i // 2
        if i % 2 == 0:
            resp, w, z, a = _s5_operators(s5_lam_re[j], s5_lam_im[j], s5_log_dt[j], s5_b_re[j],
                                          s5_b_im[j], s5_c_re[j], s5_c_im[j])
            x = x2d.reshape(bsz, seq, D_MODEL)
            y = _s5_core(_s5_pre(x, g[1]), resp, w, z, a, bsz)
            x = _s5_post(x, y, g[1], s5_d[j].astype(F32).reshape(1, D_MODEL),
                         s5_w_glu[j].astype(BF16))
        else:
            q, k, v, gate = _ret_in(x2d, g[1], pos_lanes, inv_freq, ret_w_in[j].astype(BF16))
            x = _ret_core(x2d.reshape(bsz, seq, D_MODEL), q.reshape(bsz, seq, D_MODEL),
                          k.reshape(bsz, seq, D_MODEL), v.reshape(bsz, seq, RET_VALUE_WIDTH),
                          gate.reshape(bsz, seq, RET_VALUE_WIDTH), ret_w_out[j].astype(BF16))
        x = _xattn(x, g[2], xattn_w_q[i].astype(BF16), kv[i], xattn_w_o[i].astype(BF16))
        x = _ffn(x.reshape(tokens, D_MODEL), g[3], ffn2_w_in[i].astype(BF16),
                 ffn2_w_out[i].astype(BF16), gf, i == DEPTH - 1).reshape(bsz, seq, D_MODEL)
    return x
```

```python
import functools
import math

import jax
import jax.numpy as jnp
from jax import lax
from jax.experimental import pallas as pl
from jax.experimental.pallas import tpu as pltpu

F32 = jnp.float32
BF16 = jnp.bfloat16

D_MODEL = 1024
DEPTH = 4
N_MEM = 256
D_FF = 2816
MACARON_WEIGHT = 0.5
EPS = 1e-6

S5_GROUP = 16
S5_GROUPS = D_MODEL // S5_GROUP
S5_STATE = 64
S5_CHUNK = 128
S5_COLS = S5_GROUP * S5_CHUNK
S5_TILE_CHUNKS = 4

RET_HEADS = 4
RET_QK_DIM = D_MODEL // RET_HEADS
RET_V_DIM = 2 * RET_QK_DIM
RET_VALUE_WIDTH = RET_HEADS * RET_V_DIM
RET_IN_WIDTH = 2 * D_MODEL + 2 * RET_VALUE_WIDTH
RET_CHUNK = 256
ROPE_BASE = 10000.0
ROPE_HALF = RET_QK_DIM // 2

XATTN_HEADS = 4
XATTN_DIM = D_MODEL // XATTN_HEADS

TOKEN_TILE = 512
VMEM_LIMIT = 56 * 1024 * 1024


def _params(n_axes):
    return pltpu.CompilerParams(dimension_semantics=("arbitrary",) * n_axes,
                                vmem_limit_bytes=VMEM_LIMIT)


def _resident(shape):
    zeros = (0,) * len(shape)
    return pl.BlockSpec(shape, lambda *_: zeros, pipeline_mode=pl.Buffered(1))


def _rmsnorm(x, g):
    return x * lax.rsqrt(jnp.mean(x * x, axis=-1, keepdims=True) + EPS) * g


def _dot(a, b):
    return jnp.dot(a, b, preferred_element_type=F32)


def _ffn_kernel(x_ref, g_ref, w_in_ref, w_out_ref, gn_ref, o_ref, *maybe_u_ref, next_norm):
    x = x_ref[...]
    xn = _rmsnorm(x, g_ref[...]).astype(BF16)
    h = _dot(xn, w_in_ref[...])
    a = h[:, :D_FF]
    act = (a * jax.nn.sigmoid(a) * h[:, D_FF:]).astype(BF16)
    y = x + MACARON_WEIGHT * _dot(act, w_out_ref[...])
    if next_norm == "final":
        y = _rmsnorm(y, gn_ref[...])
    o_ref[...] = y
    if next_norm == "s5":
        u_ref, = maybe_u_ref
        yn = _rmsnorm(y, gn_ref[...])
        for c in range(TOKEN_TILE // S5_CHUNK):
            u_ref[c] = yn[c * S5_CHUNK:(c + 1) * S5_CHUNK, :].T


def _ffn(x2d, g, w_in, w_out, gn, next_norm=None):
    tokens = x2d.shape[0]
    row = pl.BlockSpec((TOKEN_TILE, D_MODEL), lambda i: (i, 0))
    out_specs = [row]
    out_shape = [jax.ShapeDtypeStruct((tokens, D_MODEL), F32)]
    if next_norm == "s5":
        per_tile = TOKEN_TILE // S5_CHUNK
        out_specs.append(pl.BlockSpec((per_tile, D_MODEL, S5_CHUNK), lambda i: (i, 0, 0)))
        out_shape.append(jax.ShapeDtypeStruct((tokens // S5_CHUNK, D_MODEL, S5_CHUNK), F32))
    return pl.pallas_call(
        functools.partial(_ffn_kernel, next_norm=next_norm),
        grid=(tokens // TOKEN_TILE,),
        in_specs=[row, _resident((1, D_MODEL)), _resident((D_MODEL, 2 * D_FF)),
                  _resident((D_FF, D_MODEL)), _resident((1, D_MODEL))],
        out_specs=out_specs,
        out_shape=out_shape,
        compiler_params=_params(1),
        name="ffn",
    )(x2d, g, w_in, w_out, gn)


def _kv_kernel(mem_ref, g_ref, w_ref, o_ref):
    mn = _rmsnorm(mem_ref[...], g_ref[...]).astype(BF16)
    o_ref[0] = _dot(mn, w_ref[0]).astype(BF16)


def _memory_kv(mem2d, g, w_kv):
    rows = mem2d.shape[0]
    half = rows // 2
    return pl.pallas_call(
        _kv_kernel,
        grid=(DEPTH, 2),
        in_specs=[pl.BlockSpec((half, D_MODEL), lambda l, r: (r, 0)),
                  _resident((1, D_MODEL)),
                  pl.BlockSpec((1, D_MODEL, 2 * D_MODEL), lambda l, r: (l, 0, 0))],
        out_specs=pl.BlockSpec((1, half, 2 * D_MODEL), lambda l, r: (l, r, 0)),
        out_shape=jax.ShapeDtypeStruct((DEPTH, rows, 2 * D_MODEL), BF16),
        compiler_params=_params(2),
        name="memory_kv",
    )(mem2d, g, w_kv)


def _xattn_kernel(x_ref, g_ref, wq_ref, k_ref, v_ref, wo_ref, o_ref):
    x = x_ref[0]
    xn = _rmsnorm(x, g_ref[...]).astype(BF16)
    q = (_dot(xn, wq_ref[...]) * (XATTN_DIM ** -0.5)).astype(BF16)
    heads = []
    for h in range(XATTN_HEADS):
        sl = slice(h * XATTN_DIM, (h + 1) * XATTN_DIM)
        s = lax.dot_general(q[:, sl], k_ref[0, :, sl], (((1,), (1,)), ((), ())),
                            preferred_element_type=F32)
        e = jnp.exp(s - jnp.max(s, axis=-1, keepdims=True))
        p = e / jnp.sum(e, axis=-1, keepdims=True)
        heads.append(_dot(p.astype(BF16), v_ref[0, :, sl]).astype(BF16))
    o = jnp.concatenate(heads, axis=-1)
    o_ref[0] = x + _dot(o, wo_ref[...])


def _xattn(x3d, g, w_q, kv, w_o):
    bsz, seq, _ = x3d.shape
    row = pl.BlockSpec((1, TOKEN_TILE, D_MODEL), lambda b, i: (b, i, 0))
    return pl.pallas_call(
        _xattn_kernel,
        grid=(bsz, seq // TOKEN_TILE),
        in_specs=[row, _resident((1, D_MODEL)), _resident((D_MODEL, D_MODEL)),
                  pl.BlockSpec((1, N_MEM, D_MODEL), lambda b, i: (b, 0, 0)),
                  pl.BlockSpec((1, N_MEM, D_MODEL), lambda b, i: (b, 0, 1)),
                  _resident((D_MODEL, D_MODEL))],
        out_specs=row,
        out_shape=jax.ShapeDtypeStruct(x3d.shape, F32),
        compiler_params=_params(2),
        name="xattn",
    )(x3d, g, w_q, kv, kv, w_o)


def _ret_in_kernel(x_ref, g_ref, pos_ref, freq_ref, w_ref, q_ref, k_ref, v_ref, gate_ref):
    xn = _rmsnorm(x_ref[...], g_ref[...]).astype(BF16)
    h = _dot(xn, w_ref[...])
    ang = pos_ref[...] * freq_ref[...]
    cos = jnp.cos(ang)
    sin = jnp.sin(ang)
    k_scale = RET_QK_DIM ** -0.5
    for hd in range(RET_HEADS):
        lo = hd * RET_QK_DIM
        mid = lo + ROPE_HALF
        hi = lo + RET_QK_DIM
        q1, q2 = h[:, lo:mid], h[:, mid:hi]
        q_ref[:, lo:mid] = (q1 * cos - q2 * sin).astype(BF16)
        q_ref[:, mid:hi] = (q1 * sin + q2 * cos).astype(BF16)
        k1 = h[:, D_MODEL + lo:D_MODEL + mid] * k_scale
        k2 = h[:, D_MODEL + mid:D_MODEL + hi] * k_scale
        k_ref[:, lo:mid] = (k1 * cos - k2 * sin).astype(BF16)
        k_ref[:, mid:hi] = (k1 * sin + k2 * cos).astype(BF16)
    v_ref[...] = h[:, 2 * D_MODEL:2 * D_MODEL + RET_VALUE_WIDTH].astype(BF16)
    gate_ref[...] = h[:, 2 * D_MODEL + RET_VALUE_WIDTH:].astype(BF16)


def _ret_in(x2d, g, pos_lanes, inv_freq, w_in):
    tokens = x2d.shape[0]

    def row(width):
        return pl.BlockSpec((TOKEN_TILE, width), lambda i: (i, 0))

    def out(width):
        return jax.ShapeDtypeStruct((tokens, width), BF16)

    return pl.pallas_call(
        _ret_in_kernel,
        grid=(tokens // TOKEN_TILE,),
        in_specs=[row(D_MODEL), _resident((1, D_MODEL)), row(ROPE_HALF),
                  _resident((1, ROPE_HALF)), _resident((D_MODEL, RET_IN_WIDTH))],
        out_specs=[row(D_MODEL), row(D_MODEL), row(RET_VALUE_WIDTH), row(RET_VALUE_WIDTH)],
        out_shape=[out(D_MODEL), out(D_MODEL), out(RET_VALUE_WIDTH), out(RET_VALUE_WIDTH)],
        compiler_params=_params(1),
        name="ret_in",
    )(x2d, g, pos_lanes, inv_freq, w_in)


def _ret_core_kernel(x_ref, q_ref, k_ref, v_ref, gate_ref, w_out_ref, o_ref, state_ref, y_ref):
    @pl.when(pl.program_id(1) == 0)
    def _():
        state_ref[...] = jnp.zeros_like(state_ref)

    c = RET_CHUNK
    row_qk = lax.broadcasted_iota(jnp.int32, (c, RET_QK_DIM), 0).astype(F32)
    row_v = lax.broadcasted_iota(jnp.int32, (c, RET_V_DIM), 0).astype(F32)
    diff = (lax.broadcasted_iota(jnp.int32, (c, c), 0)
            - lax.broadcasted_iota(jnp.int32, (c, c), 1))
    lag = jnp.maximum(diff, 0).astype(F32)
    for hd in range(RET_HEADS):
        log_gamma = math.log(1.0 - 2.0 ** (-5 - hd))
        d_mat = jnp.where(diff >= 0, jnp.exp(log_gamma * lag), 0.0)
        xi = jnp.exp(log_gamma * (row_v + 1.0))
        zeta = jnp.exp(log_gamma * (c - 1.0 - row_qk))
        gamma_chunk = math.exp(log_gamma * c)
        qk = slice(hd * RET_QK_DIM, (hd + 1) * RET_QK_DIM)
        vv = slice(hd * RET_V_DIM, (hd + 1) * RET_V_DIM)
        for ci in range(TOKEN_TILE // c):
            rows = slice(ci * c, (ci + 1) * c)
            qc = q_ref[0, rows, qk]
            kc = k_ref[0, rows, qk]
            vc = v_ref[0, rows, vv]
            state = state_ref[hd]
            inner = lax.dot_general(qc, kc, (((1,), (1,)), ((), ())),
                                    preferred_element_type=F32) * d_mat
            o = _dot(inner.astype(BF16), vc) + _dot(qc, state.astype(BF16)) * xi
            kz = (kc.astype(F32) * zeta).astype(BF16)
            state_ref[hd] = state * gamma_chunk + lax.dot_general(
                kz, vc, (((0,), (0,)), ((), ())), preferred_element_type=F32)
            mu = jnp.mean(o, axis=-1, keepdims=True)
            dev = o - mu
            var = jnp.mean(dev * dev, axis=-1, keepdims=True)
            gate = gate_ref[0, rows, vv].astype(F32)
            y = gate * jax.nn.sigmoid(gate) * (dev * lax.rsqrt(var + EPS))
            y_ref[rows, vv] = y.astype(BF16)
    o_ref[0] = x_ref[0] + _dot(y_ref[...], w_out_ref[...])


def _ret_core(x3d, q, k, v, gate, w_out):
    bsz, seq, _ = x3d.shape

    def row(width):
        return pl.BlockSpec((1, TOKEN_TILE, width), lambda b, i: (b, i, 0))

    return pl.pallas_call(
        _ret_core_kernel,
        grid=(bsz, seq // TOKEN_TILE),
        in_specs=[row(D_MODEL), row(D_MODEL), row(D_MODEL), row(RET_VALUE_WIDTH),
                  row(RET_VALUE_WIDTH), _resident((RET_VALUE_WIDTH, D_MODEL))],
        out_specs=row(D_MODEL),
        out_shape=jax.ShapeDtypeStruct(x3d.shape, F32),
        scratch_shapes=[pltpu.VMEM((RET_HEADS, RET_QK_DIM, RET_V_DIM), F32),
                        pltpu.VMEM((TOKEN_TILE, RET_VALUE_WIDTH), BF16)],
        compiler_params=_params(2),
        name="ret_core",
    )(x3d, q, k, v, gate, w_out)


def _s5_operators(lam_re, lam_im, log_dt, b_re, b_im, c_re, c_im):
    n = S5_CHUNK
    hp = lax.Precision.HIGHEST
    lr, li = lam_re.astype(F32), lam_im.astype(F32)
    dt = jnp.exp(log_dt.astype(F32))[:, None]
    mag = jnp.exp(lr * dt)
    ar = mag * jnp.cos(li * dt)
    ai = mag * jnp.sin(li * dt)
    den = lr * lr + li * li
    nr = ar - 1.0
    zr = (nr * lr + ai * li) / den
    zi = (ai * lr - nr * li) / den
    bbr = zr[..., None] * b_re - zi[..., None] * b_im
    bbi = zr[..., None] * b_im + zi[..., None] * b_re
    steps = jnp.arange(n + 1, dtype=F32)[None, :, None]
    pmag = jnp.exp((lr * dt)[:, None, :] * steps)
    phase = (li * dt)[:, None, :] * steps
    pr = pmag * jnp.cos(phase)
    pi = pmag * jnp.sin(phase)
    clr = c_re[:, :, None, :] * pr[:, None] - c_im[:, :, None, :] * pi[:, None]
    cli = c_re[:, :, None, :] * pi[:, None] + c_im[:, :, None, :] * pr[:, None]
    resp = (jnp.einsum('gotp,gph->ghot', clr[:, :, :n], bbr, precision=hp)
            - jnp.einsum('gotp,gph->ghot', cli[:, :, :n], bbi, precision=hp))
    resp = resp.reshape(S5_GROUPS, S5_GROUP * S5_GROUP, n)
    prr = pr[:, n - 1::-1][:, None]
    pir = pi[:, n - 1::-1][:, None]
    bbr_t = bbr.transpose(0, 2, 1)[:, :, None, :]
    bbi_t = bbi.transpose(0, 2, 1)[:, :, None, :]
    wr = prr * bbr_t - pir * bbi_t
    wi = prr * bbi_t + pir * bbr_t
    w = jnp.concatenate([wr, wi, wi, wr], axis=-1).reshape(S5_GROUPS, S5_COLS, 4 * S5_STATE)
    zr_ = clr[:, :, 1:].transpose(0, 3, 1, 2)
    zi_ = -cli[:, :, 1:].transpose(0, 3, 1, 2)
    z = jnp.concatenate([zr_, zi_], axis=1).reshape(S5_GROUPS, 2 * S5_STATE, S5_COLS)
    alr, ali = pr[:, n], pi[:, n]
    zero = jnp.zeros_like(alr)
    a = jnp.stack([jnp.concatenate([alr, alr], -1), jnp.concatenate([-ali, ali], -1),
                   jnp.concatenate([ali, -ali], -1)] + [jnp.concatenate([zero, zero], -1)] * 5,
                  axis=1)
    return resp, w.astype(BF16), z.astype(BF16), a


def _channel_rows(ref, ch):
    rows, channels, width = ref.shape
    return ref.reshape(rows * channels, width).at[pl.ds(ch, rows, stride=channels), :]


def _toeplitz_block(r_ref, row, causal):
    n = S5_CHUNK
    lags = jnp.broadcast_to(r_ref[0, pl.ds(row, 1), :], (n, n))
    blk = pltpu.roll(lags, 0, 1, stride=1, stride_axis=0)
    return jnp.where(causal, blk, 0.0).astype(BF16)


def _s5_core_kernel(u_ref, r0_ref, r_next_ref, w_ref, z_ref, a_ref, y_ref, t0_ref, t1_ref, us_ref,
                    v_ref, xin_ref, *, bsz, n_chunks):
    n = S5_CHUNK
    g = pl.program_id(0)
    causal = (lax.broadcasted_iota(jnp.int32, (n, n), 1)
              >= lax.broadcasted_iota(jnp.int32, (n, n), 0))

    @pl.when(g == 0)
    def _():
        def build_rows(hi, carry):
            for ho in range(S5_GROUP):
                t0_ref[pl.ds(pl.multiple_of(hi * n, n), n), ho * n:(ho + 1) * n] = (
                    _toeplitz_block(r0_ref, hi * S5_GROUP + ho, causal))
            return carry

        lax.fori_loop(0, S5_GROUP, build_rows, 0)

    @pl.when(g % 2 == 0)
    def _():
        _s5_group(u_ref, r_next_ref, w_ref, z_ref, a_ref, y_ref, t0_ref, t1_ref, us_ref, v_ref,
                  xin_ref, causal, bsz, n_chunks)

    @pl.when(g % 2 == 1)
    def _():
        _s5_group(u_ref, r_next_ref, w_ref, z_ref, a_ref, y_ref, t1_ref, t0_ref, us_ref, v_ref,
                  xin_ref, causal, bsz, n_chunks)


def _s5_group(u_ref, r_next_ref, w_ref, z_ref, a_ref, y_ref, t_ref, t_next_ref, us_ref, v_ref,
              xin_ref, causal, bsz, n_chunks):
    n = S5_CHUNK
    for hi in range(S5_GROUP):
        for ho in range(S5_GROUP):
            t_next_ref[hi * n:(hi + 1) * n, ho * n:(ho + 1) * n] = (
                _toeplitz_block(r_next_ref, hi * S5_GROUP + ho, causal))

    for hi in range(S5_GROUP):
        us_ref[:, hi * n:(hi + 1) * n] = _channel_rows(u_ref, hi)[...].astype(BF16)
    u = us_ref[...]
    half = 2 * S5_STATE
    v = _dot(u, w_ref[0])
    v_ref[0] = v[:, :half]
    v_ref[1] = v[:, half:]
    a_same = a_ref[0, 0:1, :]
    a_cross_p = a_ref[0, 1:2, :]
    a_cross_q = a_ref[0, 2:3, :]
    p = jnp.zeros((bsz, half), F32)
    q = jnp.zeros((bsz, half), F32)
    for c in range(n_chunks):
        xin_ref[pl.ds(c, bsz, stride=n_chunks), :] = p
        p, q = (a_same * p + a_cross_p * q + v_ref[0, pl.ds(c, bsz, stride=n_chunks), :],
                a_same * q + a_cross_q * p + v_ref[1, pl.ds(c, bsz, stride=n_chunks), :])
    y = _dot(u, t_ref[...]) + _dot(xin_ref[...].astype(BF16), z_ref[0])
    for ho in range(S5_GROUP):
        _channel_rows(y_ref, ho)[...] = y[:, ho * n:(ho + 1) * n]


def _s5_core(u3d, resp, w, z, a, bsz):
    rows = u3d.shape[0]
    chunk_rows = pl.BlockSpec((rows, S5_GROUP, S5_CHUNK), lambda g: (0, g, 0))
    return pl.pallas_call(
        functools.partial(_s5_core_kernel, bsz=bsz, n_chunks=rows // bsz),
        grid=(S5_GROUPS,),
        in_specs=[chunk_rows,
                  pl.BlockSpec((1, S5_GROUP * S5_GROUP, S5_CHUNK), lambda g: (0, 0, 0)),
                  pl.BlockSpec((1, S5_GROUP * S5_GROUP, S5_CHUNK),
                               lambda g: (jnp.minimum(g + 1, S5_GROUPS - 1), 0, 0)),
                  pl.BlockSpec((1, S5_COLS, 4 * S5_STATE), lambda g: (g, 0, 0)),
                  pl.BlockSpec((1, 2 * S5_STATE, S5_COLS), lambda g: (g, 0, 0)),
                  pl.BlockSpec((1, 8, 2 * S5_STATE), lambda g: (g, 0, 0))],
        out_specs=chunk_rows,
        out_shape=jax.ShapeDtypeStruct((rows, D_MODEL, S5_CHUNK), F32),
        scratch_shapes=[pltpu.VMEM((S5_COLS, S5_COLS), BF16),
                        pltpu.VMEM((S5_COLS, S5_COLS), BF16),
                        pltpu.VMEM((rows, S5_COLS), BF16),
                        pltpu.VMEM((2, rows, 2 * S5_STATE), F32),
                        pltpu.VMEM((rows, 2 * S5_STATE), F32)],
        compiler_params=_params(1),
        name="s5_core",
    )(u3d, resp, resp, w, z, a)


def _gelu_tanh(x):
    return 0.5 * x * (1.0 + jnp.tanh(math.sqrt(2.0 / math.pi) * (x + 0.044715 * (x * x * x))))


def _s5_post_kernel(x_ref, y_ref, g_ref, d_ref, w_ref, o_ref, act_ref):
    x = x_ref[0]
    skip = d_ref[...] * _rmsnorm(x, g_ref[...])
    for c in range(S5_TILE_CHUNKS):
        rows = slice(c * S5_CHUNK, (c + 1) * S5_CHUNK)
        act_ref[rows, :] = _gelu_tanh(y_ref[c].T + skip[rows, :]).astype(BF16)
    z = _dot(act_ref[...], w_ref[...])
    o_ref[0] = x + z[:, :D_MODEL] * jax.nn.sigmoid(z[:, D_MODEL:])


def _s5_post(x3d, y3d, g, d_skip, w_glu):
    bsz, seq, _ = x3d.shape
    tile = S5_TILE_CHUNKS * S5_CHUNK
    steps = seq // tile
    row = pl.BlockSpec((1, tile, D_MODEL), lambda b, i: (b, i, 0))
    return pl.pallas_call(
        _s5_post_kernel,
        grid=(bsz, steps),
        in_specs=[row,
                  pl.BlockSpec((S5_TILE_CHUNKS, D_MODEL, S5_CHUNK), lambda b, i: (b * steps + i, 0, 0)),
                  _resident((1, D_MODEL)), _resident((1, D_MODEL)),
                  _resident((D_MODEL, 2 * D_MODEL))],
        out_specs=row,
        out_shape=jax.ShapeDtypeStruct(x3d.shape, F32),
        scratch_shapes=[pltpu.VMEM((tile, D_MODEL), BF16)],
        compiler_params=_params(2),
        name="s5_post",
    )(x3d, y3d, g, d_skip, w_glu)


def kernel(x, mem, positions, norm_gains, mem_norm, final_norm, ffn1_w_in, ffn1_w_out, ffn2_w_in, ffn2_w_out, s5_lam_re, s5_lam_im, s5_log_dt, s5_b_re, s5_b_im, s5_c_re, s5_c_im, s5_d, s5_w_glu, ret_w_in, ret_w_out, xattn_w_q, xattn_w_kv, xattn_w_o):
    bsz, seq, _ = x.shape
    tokens = bsz * seq
    gains = norm_gains.astype(F32)
    gf = final_norm.astype(F32).reshape(1, D_MODEL)

    kv = _memory_kv(mem.reshape(bsz * N_MEM, D_MODEL), mem_norm.astype(F32).reshape(1, D_MODEL),
                    xattn_w_kv.astype(BF16))
    kv = kv.reshape(DEPTH, bsz, N_MEM, 2 * D_MODEL)
    pos_lanes = jnp.broadcast_to(positions.astype(F32).reshape(tokens, 1), (tokens, ROPE_HALF))
    inv_freq = (1.0 / (ROPE_BASE ** jnp.linspace(0.0, 1.0, ROPE_HALF, dtype=F32))).reshape(1, ROPE_HALF)

    for i in range(DEPTH):
        g = [gains[i, j].reshape(1, D_MODEL) for j in range(4)]
        ffn1 = functools.partial(_ffn, x.reshape(tokens, D_MODEL), g[0], ffn1_w_in[i].astype(BF16),
                                 ffn1_w_out[i].astype(BF16))
        j = i // 2
        if i % 2 == 0:
            resp, w, z, a = _s5_operators(s5_lam_re[j], s5_lam_im[j], s5_log_dt[j], s5_b_re[j],
                                          s5_b_im[j], s5_c_re[j], s5_c_im[j])
            x2d, u = ffn1(g[1], "s5")
            x = x2d.reshape(bsz, seq, D_MODEL)
            y = _s5_core(u, resp, w, z, a, bsz)
            x = _s5_post(x, y, g[1], s5_d[j].astype(F32).reshape(1, D_MODEL),
                         s5_w_glu[j].astype(BF16))
        else:
            x2d, = ffn1(gf)
            q, k, v, gate = _ret_in(x2d, g[1], pos_lanes, inv_freq, ret_w_in[j].astype(BF16))
            x = _ret_core(x2d.reshape(bsz, seq, D_MODEL), q.reshape(bsz, seq, D_MODEL),
                          k.reshape(bsz, seq, D_MODEL), v.reshape(bsz, seq, RET_VALUE_WIDTH),
                          gate.reshape(bsz, seq, RET_VALUE_WIDTH), ret_w_out[j].astype(BF16))
        x = _xattn(x, g[2], xattn_w_q[i].astype(BF16), kv[i], xattn_w_o[i].astype(BF16))
        x, = _ffn(x.reshape(tokens, D_MODEL), g[3], ffn2_w_in[i].astype(BF16),
                  ffn2_w_out[i].astype(BF16), gf, "final" if i == DEPTH - 1 else None)
        x = x.reshape(bsz, seq, D_MODEL)
    return x
```

```python
import functools
import math

import jax
import jax.numpy as jnp
from jax import lax
from jax.experimental import pallas as pl
from jax.experimental.pallas import tpu as pltpu

F32 = jnp.float32
BF16 = jnp.bfloat16

D_MODEL = 1024
DEPTH = 4
N_MEM = 256
D_FF = 2816
MACARON_WEIGHT = 0.5
EPS = 1e-6

S5_GROUP = 16
S5_GROUPS = D_MODEL // S5_GROUP
S5_STATE = 64
S5_CHUNK = 128
S5_COLS = S5_GROUP * S5_CHUNK
S5_TILE_CHUNKS = 8

RET_HEADS = 4
RET_QK_DIM = D_MODEL // RET_HEADS
RET_V_DIM = 2 * RET_QK_DIM
RET_VALUE_WIDTH = RET_HEADS * RET_V_DIM
RET_IN_WIDTH = 2 * D_MODEL + 2 * RET_VALUE_WIDTH
RET_CHUNK = 256
ROPE_BASE = 10000.0
ROPE_HALF = RET_QK_DIM // 2

XATTN_HEADS = 4
XATTN_DIM = D_MODEL // XATTN_HEADS

TOKEN_TILE = 512
XATTN_TILE = 1024
RET_IN_TILE = 1024
VMEM_LIMIT = 56 * 1024 * 1024


def _params(n_axes):
    return pltpu.CompilerParams(dimension_semantics=("arbitrary",) * n_axes,
                                vmem_limit_bytes=VMEM_LIMIT)


def _resident(shape):
    zeros = (0,) * len(shape)
    return pl.BlockSpec(shape, lambda *_: zeros, pipeline_mode=pl.Buffered(1))


def _rmsnorm(x, g):
    return x * lax.rsqrt(jnp.mean(x * x, axis=-1, keepdims=True) + EPS) * g


def _dot(a, b):
    return jnp.dot(a, b, preferred_element_type=F32)


def _ffn_kernel(x_ref, g_ref, w_in_ref, w_out_ref, gn_ref, o_ref, *extra_refs, extra):
    x = x_ref[...]
    xn = _rmsnorm(x, g_ref[...]).astype(BF16)
    h = _dot(xn, w_in_ref[...])
    a = h[:, :D_FF]
    act = (a * jax.nn.sigmoid(a) * h[:, D_FF:]).astype(BF16)
    y = x + MACARON_WEIGHT * _dot(act, w_out_ref[...])
    if extra == "final":
        y = _rmsnorm(y, gn_ref[...])
    o_ref[...] = y
    if extra == "s5":
        u_ref, = extra_refs
        yn = _rmsnorm(y, gn_ref[...])
        for c in range(TOKEN_TILE // S5_CHUNK):
            u_ref[c] = yn[c * S5_CHUNK:(c + 1) * S5_CHUNK, :].T


def _ffn(x2d, g, w_in, w_out, gn, extra=None):
    tokens = x2d.shape[0]
    row = pl.BlockSpec((TOKEN_TILE, D_MODEL), lambda i: (i, 0))
    out_specs = [row]
    out_shape = [jax.ShapeDtypeStruct((tokens, D_MODEL), F32)]
    if extra == "s5":
        per_tile = TOKEN_TILE // S5_CHUNK
        out_specs.append(pl.BlockSpec((per_tile, D_MODEL, S5_CHUNK), lambda i: (i, 0, 0)))
        out_shape.append(jax.ShapeDtypeStruct((tokens // S5_CHUNK, D_MODEL, S5_CHUNK), F32))
    return pl.pallas_call(
        functools.partial(_ffn_kernel, extra=extra),
        grid=(tokens // TOKEN_TILE,),
        in_specs=[row, _resident((1, D_MODEL)), _resident((D_MODEL, 2 * D_FF)),
                  _resident((D_FF, D_MODEL)), _resident((1, D_MODEL))],
        out_specs=out_specs,
        out_shape=out_shape,
        compiler_params=_params(1),
        name="ffn",
    )(x2d, g, w_in, w_out, gn)


def _kv_kernel(mem_ref, g_ref, w_ref, o_ref):
    mn = _rmsnorm(mem_ref[...], g_ref[...]).astype(BF16)
    o_ref[0] = _dot(mn, w_ref[0]).astype(BF16)


def _memory_kv(mem2d, g, w_kv):
    rows = mem2d.shape[0]
    half = rows // 2
    return pl.pallas_call(
        _kv_kernel,
        grid=(DEPTH, 2),
        in_specs=[pl.BlockSpec((half, D_MODEL), lambda l, r: (r, 0)),
                  _resident((1, D_MODEL)),
                  pl.BlockSpec((1, D_MODEL, 2 * D_MODEL), lambda l, r: (l, 0, 0))],
        out_specs=pl.BlockSpec((1, half, 2 * D_MODEL), lambda l, r: (l, r, 0)),
        out_shape=jax.ShapeDtypeStruct((DEPTH, rows, 2 * D_MODEL), BF16),
        compiler_params=_params(2),
        name="memory_kv",
    )(mem2d, g, w_kv)


def _xattn_kernel(x_ref, g_ref, wq_ref, k_ref, v_ref, wo_ref, o_ref):
    for sub in range(XATTN_TILE // TOKEN_TILE):
        rows = slice(sub * TOKEN_TILE, (sub + 1) * TOKEN_TILE)
        x = x_ref[0, rows, :]
        xn = _rmsnorm(x, g_ref[...]).astype(BF16)
        q = (_dot(xn, wq_ref[...]) * (XATTN_DIM ** -0.5)).astype(BF16)
        heads = []
        for h in range(XATTN_HEADS):
            sl = slice(h * XATTN_DIM, (h + 1) * XATTN_DIM)
            s = lax.dot_general(q[:, sl], k_ref[0, :, sl], (((1,), (1,)), ((), ())),
                                preferred_element_type=F32)
            e = jnp.exp(s - jnp.max(s, axis=-1, keepdims=True))
            p = e / jnp.sum(e, axis=-1, keepdims=True)
            heads.append(_dot(p.astype(BF16), v_ref[0, :, sl]).astype(BF16))
        o = jnp.concatenate(heads, axis=-1)
        o_ref[0, rows, :] = x + _dot(o, wo_ref[...])


def _xattn(x3d, g, w_q, kv, w_o):
    bsz, seq, _ = x3d.shape
    row = pl.BlockSpec((1, XATTN_TILE, D_MODEL), lambda b, i: (b, i, 0))
    return pl.pallas_call(
        _xattn_kernel,
        grid=(bsz, seq // XATTN_TILE),
        in_specs=[row, _resident((1, D_MODEL)), _resident((D_MODEL, D_MODEL)),
                  pl.BlockSpec((1, N_MEM, D_MODEL), lambda b, i: (b, 0, 0)),
                  pl.BlockSpec((1, N_MEM, D_MODEL), lambda b, i: (b, 0, 1)),
                  _resident((D_MODEL, D_MODEL))],
        out_specs=row,
        out_shape=jax.ShapeDtypeStruct(x3d.shape, F32),
        compiler_params=_params(2),
        name="xattn",
    )(x3d, g, w_q, kv, kv, w_o)


def _ret_decay_tables():
    c = RET_CHUNK
    log_gamma = jnp.log(1.0 - jnp.exp2(-5.0 - jnp.arange(RET_HEADS, dtype=F32)))
    idx = jnp.arange(c)
    diff = idx[:, None] - idx[None, :]
    d_mat = jnp.where(diff >= 0, jnp.exp(log_gamma[:, None, None]
                                         * jnp.maximum(diff, 0).astype(F32)), 0.0)
    xi = jnp.exp(log_gamma[:, None] * (idx + 1).astype(F32))
    zeta = jnp.exp(log_gamma[:, None] * (c - 1 - idx).astype(F32))
    xi = jnp.broadcast_to(xi[:, :, None], (RET_HEADS, c, 128))
    zeta = jnp.broadcast_to(zeta.T[:, :, None], (c, RET_HEADS, RET_QK_DIM)).reshape(c, D_MODEL)
    return d_mat, xi, zeta


def _ret_in_kernel(x_ref, g_ref, rope_a_ref, rope_b_ref, zeta_ref, w_ref,
                   q_ref, k_ref, kz_ref, v_ref, sg_ref, *rope_out_refs, make_rope):
    if make_rope:
        cos_ref, sin_ref = rope_out_refs
        ang = rope_a_ref[...] * rope_b_ref[...]
        cos_ref[...] = jnp.cos(ang)
        sin_ref[...] = jnp.sin(ang)
    else:
        cos_ref, sin_ref = rope_a_ref, rope_b_ref
    k_scale = RET_QK_DIM ** -0.5
    for sub in range(RET_IN_TILE // TOKEN_TILE):
        base = sub * TOKEN_TILE
        tok = slice(base, base + TOKEN_TILE)
        xn = _rmsnorm(x_ref[tok, :], g_ref[...]).astype(BF16)
        h = _dot(xn, w_ref[...])
        cos = cos_ref[tok, :]
        sin = sin_ref[tok, :]
        for hd in range(RET_HEADS):
            lo = hd * RET_QK_DIM
            mid = lo + ROPE_HALF
            hi = lo + RET_QK_DIM
            q1, q2 = h[:, lo:mid], h[:, mid:hi]
            q_ref[tok, lo:mid] = (q1 * cos - q2 * sin).astype(BF16)
            q_ref[tok, mid:hi] = (q1 * sin + q2 * cos).astype(BF16)
            k1 = h[:, D_MODEL + lo:D_MODEL + mid] * k_scale
            k2 = h[:, D_MODEL + mid:D_MODEL + hi] * k_scale
            for dst, kr in ((slice(lo, mid), k1 * cos - k2 * sin),
                            (slice(mid, hi), k1 * sin + k2 * cos)):
                k_ref[tok, dst] = kr.astype(BF16)
                for ci in range(TOKEN_TILE // RET_CHUNK):
                    rows = slice(ci * RET_CHUNK, (ci + 1) * RET_CHUNK)
                    kz_ref[base + ci * RET_CHUNK:base + (ci + 1) * RET_CHUNK, dst] = (
                        kr[rows, :] * zeta_ref[:, dst]).astype(BF16)
        v_ref[tok, :] = h[:, 2 * D_MODEL:2 * D_MODEL + RET_VALUE_WIDTH].astype(BF16)
        gate = h[:, 2 * D_MODEL + RET_VALUE_WIDTH:]
        sg_ref[tok, :] = (gate * jax.nn.sigmoid(gate)).astype(BF16)


def _ret_in(x2d, g, rope, zeta, w_in, make_rope):
    tokens = x2d.shape[0]

    def row(width):
        return pl.BlockSpec((RET_IN_TILE, width), lambda i: (i, 0))

    def out(width, dtype=BF16):
        return jax.ShapeDtypeStruct((tokens, width), dtype)

    rope_b = _resident((1, ROPE_HALF)) if make_rope else row(ROPE_HALF)
    rope_out = 2 if make_rope else 0
    return pl.pallas_call(
        functools.partial(_ret_in_kernel, make_rope=make_rope),
        grid=(tokens // RET_IN_TILE,),
        in_specs=[row(D_MODEL), _resident((1, D_MODEL)), row(ROPE_HALF), rope_b,
                  _resident((RET_CHUNK, D_MODEL)), _resident((D_MODEL, RET_IN_WIDTH))],
        out_specs=[row(D_MODEL), row(D_MODEL), row(D_MODEL), row(RET_VALUE_WIDTH),
                   row(RET_VALUE_WIDTH)] + [row(ROPE_HALF)] * rope_out,
        out_shape=[out(D_MODEL), out(D_MODEL), out(D_MODEL), out(RET_VALUE_WIDTH),
                   out(RET_VALUE_WIDTH)] + [out(ROPE_HALF, F32)] * rope_out,
        compiler_params=_params(1),
        name="ret_in",
    )(x2d, g, *rope, zeta, w_in)


def _ret_core_kernel(x_ref, q_ref, k_ref, kz_ref, v_ref, sg_ref, d_ref, xi_ref, w_out_ref, o_ref,
                     state_ref, y_ref):
    @pl.when(pl.program_id(1) == 0)
    def _():
        state_ref[...] = jnp.zeros_like(state_ref)

    c = RET_CHUNK
    for hd in range(RET_HEADS):
        gamma_chunk = (1.0 - 2.0 ** (-5 - hd)) ** c
        xi = jnp.concatenate([xi_ref[hd]] * (RET_V_DIM // 128), axis=-1)
        qk = slice(hd * RET_QK_DIM, (hd + 1) * RET_QK_DIM)
        vv = slice(hd * RET_V_DIM, (hd + 1) * RET_V_DIM)
        for ci in range(TOKEN_TILE // c):
            rows = slice(ci * c, (ci + 1) * c)
            qc = q_ref[0, rows, qk]
            vc = v_ref[0, rows, vv]
            state = state_ref[hd]
            inner = lax.dot_general(qc, k_ref[0, rows, qk], (((1,), (1,)), ((), ())),
                                    preferred_element_type=F32) * d_ref[hd]
            o = _dot(inner.astype(BF16), vc) + _dot(qc, state.astype(BF16)) * xi
            state_ref[hd] = state * gamma_chunk + lax.dot_general(
                kz_ref[0, rows, qk], vc, (((0,), (0,)), ((), ())), preferred_element_type=F32)
            mu = jnp.mean(o, axis=-1, keepdims=True)
            dev = o - mu
            var = jnp.mean(dev * dev, axis=-1, keepdims=True)
            y = sg_ref[0, rows, vv].astype(F32) * (dev * lax.rsqrt(var + EPS))
            y_ref[rows, vv] = y.astype(BF16)
    o_ref[0] = x_ref[0] + _dot(y_ref[...], w_out_ref[...])


def _ret_core(x3d, q, k, kz, v, sg, d_mat, xi, w_out):
    bsz, seq, _ = x3d.shape

    def row(width):
        return pl.BlockSpec((1, TOKEN_TILE, width), lambda b, i: (b, i, 0))

    return pl.pallas_call(
        _ret_core_kernel,
        grid=(bsz, seq // TOKEN_TILE),
        in_specs=[row(D_MODEL), row(D_MODEL), row(D_MODEL), row(D_MODEL), row(RET_VALUE_WIDTH),
                  row(RET_VALUE_WIDTH), _resident((RET_HEADS, RET_CHUNK, RET_CHUNK)),
                  _resident((RET_HEADS, RET_CHUNK, 128)), _resident((RET_VALUE_WIDTH, D_MODEL))],
        out_specs=row(D_MODEL),
        out_shape=jax.ShapeDtypeStruct(x3d.shape, F32),
        scratch_shapes=[pltpu.VMEM((RET_HEADS, RET_QK_DIM, RET_V_DIM), F32),
                        pltpu.VMEM((TOKEN_TILE, RET_VALUE_WIDTH), BF16)],
        compiler_params=_params(2),
        name="ret_core",
    )(x3d, q, k, kz, v, sg, d_mat, xi, w_out)


def _s5_operators(lam_re, lam_im, log_dt, b_re, b_im, c_re, c_im):
    n = S5_CHUNK
    hp = lax.Precision.HIGHEST
    lr, li = lam_re.astype(F32), lam_im.astype(F32)
    dt = jnp.exp(log_dt.astype(F32))[:, None]
    mag = jnp.exp(lr * dt)
    ar = mag * jnp.cos(li * dt)
    ai = mag * jnp.sin(li * dt)
    den = lr * lr + li * li
    nr = ar - 1.0
    zr = (nr * lr + ai * li) / den
    zi = (ai * lr - nr * li) / den
    bbr = zr[..., None] * b_re - zi[..., None] * b_im
    bbi = zr[..., None] * b_im + zi[..., None] * b_re
    steps = jnp.arange(n + 1, dtype=F32)[None, :, None]
    pmag = jnp.exp((lr * dt)[:, None, :] * steps)
    phase = (li * dt)[:, None, :] * steps
    pr = pmag * jnp.cos(phase)
    pi = pmag * jnp.sin(phase)
    clr = c_re[:, :, None, :] * pr[:, None] - c_im[:, :, None, :] * pi[:, None]
    cli = c_re[:, :, None, :] * pi[:, None] + c_im[:, :, None, :] * pr[:, None]
    resp = (jnp.einsum('gotp,gph->ghot', clr[:, :, :n], bbr, precision=hp)
            - jnp.einsum('gotp,gph->ghot', cli[:, :, :n], bbi, precision=hp))
    resp = resp.reshape(S5_GROUPS, S5_GROUP * S5_GROUP, n)
    prr = pr[:, n - 1::-1][:, None]
    pir = pi[:, n - 1::-1][:, None]
    bbr_t = bbr.transpose(0, 2, 1)[:, :, None, :]
    bbi_t = bbi.transpose(0, 2, 1)[:, :, None, :]
    wr = prr * bbr_t - pir * bbi_t
    wi = prr * bbi_t + pir * bbr_t
    w = jnp.concatenate([wr, wi, wi, wr], axis=-1).reshape(S5_GROUPS, S5_COLS, 4 * S5_STATE)
    zr_ = clr[:, :, 1:].transpose(0, 3, 1, 2)
    zi_ = -cli[:, :, 1:].transpose(0, 3, 1, 2)
    z = jnp.concatenate([zr_, zi_], axis=1).reshape(S5_GROUPS, 2 * S5_STATE, S5_COLS)
    alr, ali = pr[:, n], pi[:, n]
    zero = jnp.zeros_like(alr)
    a = jnp.stack([jnp.concatenate([alr, alr], -1), jnp.concatenate([-ali, ali], -1),
                   jnp.concatenate([ali, -ali], -1)] + [jnp.concatenate([zero, zero], -1)] * 5,
                  axis=1)
    return resp, w.astype(BF16), z.astype(BF16), a


def _channel_rows(ref, ch):
    rows, channels, width = ref.shape
    return ref.reshape(rows * channels, width).at[pl.ds(ch, rows, stride=channels), :]


def _toeplitz_block(r_ref, row, causal):
    n = S5_CHUNK
    lags = jnp.broadcast_to(r_ref[0, pl.ds(row, 1), :], (n, n))
    blk = pltpu.roll(lags, 0, 1, stride=1, stride_axis=0)
    return jnp.where(causal, blk, 0.0).astype(BF16)


def _s5_core_kernel(u_ref, r0_ref, r_next_ref, w_ref, z_ref, a_ref, y_ref, t0_ref, t1_ref, us_ref,
                    v_ref, xin_ref, *, bsz, n_chunks):
    n = S5_CHUNK
    g = pl.program_id(0)
    causal = (lax.broadcasted_iota(jnp.int32, (n, n), 1)
              >= lax.broadcasted_iota(jnp.int32, (n, n), 0))

    @pl.when(g == 0)
    def _():
        def build_rows(hi, carry):
            for ho in range(S5_GROUP):
                t0_ref[pl.ds(pl.multiple_of(hi * n, n), n), ho * n:(ho + 1) * n] = (
                    _toeplitz_block(r0_ref, hi * S5_GROUP + ho, causal))
            return carry

        lax.fori_loop(0, S5_GROUP, build_rows, 0)

    @pl.when(g % 2 == 0)
    def _():
        _s5_group(u_ref, r_next_ref, w_ref, z_ref, a_ref, y_ref, t0_ref, t1_ref, us_ref, v_ref,
                  xin_ref, causal, bsz, n_chunks)

    @pl.when(g % 2 == 1)
    def _():
        _s5_group(u_ref, r_next_ref, w_ref, z_ref, a_ref, y_ref, t1_ref, t0_ref, us_ref, v_ref,
                  xin_ref, causal, bsz, n_chunks)


def _s5_group(u_ref, r_next_ref, w_ref, z_ref, a_ref, y_ref, t_ref, t_next_ref, us_ref, v_ref,
              xin_ref, causal, bsz, n_chunks):
    n = S5_CHUNK
    for hi in range(S5_GROUP):
        for ho in range(S5_GROUP):
            t_next_ref[hi * n:(hi + 1) * n, ho * n:(ho + 1) * n] = (
                _toeplitz_block(r_next_ref, hi * S5_GROUP + ho, causal))

    for hi in range(S5_GROUP):
        us_ref[:, hi * n:(hi + 1) * n] = _channel_rows(u_ref, hi)[...].astype(BF16)
    u = us_ref[...]
    half = 2 * S5_STATE
    v = _dot(u, w_ref[0])
    v_ref[0] = v[:, :half]
    v_ref[1] = v[:, half:]
    a_same = a_ref[0, 0:1, :]
    a_cross_p = a_ref[0, 1:2, :]
    a_cross_q = a_ref[0, 2:3, :]
    p = jnp.zeros((bsz, half), F32)
    q = jnp.zeros((bsz, half), F32)
    for c in range(n_chunks):
        xin_ref[pl.ds(c, bsz, stride=n_chunks), :] = p
        p, q = (a_same * p + a_cross_p * q + v_ref[0, pl.ds(c, bsz, stride=n_chunks), :],
                a_same * q + a_cross_q * p + v_ref[1, pl.ds(c, bsz, stride=n_chunks), :])
    y = _dot(u, t_ref[...]) + _dot(xin_ref[...].astype(BF16), z_ref[0])
    for ho in range(S5_GROUP):
        _channel_rows(y_ref, ho)[...] = y[:, ho * n:(ho + 1) * n]


def _s5_core(u3d, resp, w, z, a, bsz):
    rows = u3d.shape[0]
    chunk_rows = pl.BlockSpec((rows, S5_GROUP, S5_CHUNK), lambda g: (0, g, 0))
    return pl.pallas_call(
        functools.partial(_s5_core_kernel, bsz=bsz, n_chunks=rows // bsz),
        grid=(S5_GROUPS,),
        in_specs=[chunk_rows,
                  pl.BlockSpec((1, S5_GROUP * S5_GROUP, S5_CHUNK), lambda g: (0, 0, 0)),
                  pl.BlockSpec((1, S5_GROUP * S5_GROUP, S5_CHUNK),
                               lambda g: (jnp.minimum(g + 1, S5_GROUPS - 1), 0, 0)),
                  pl.BlockSpec((1, S5_COLS, 4 * S5_STATE), lambda g: (g, 0, 0)),
                  pl.BlockSpec((1, 2 * S5_STATE, S5_COLS), lambda g: (g, 0, 0)),
                  pl.BlockSpec((1, 8, 2 * S5_STATE), lambda g: (g, 0, 0))],
        out_specs=chunk_rows,
        out_shape=jax.ShapeDtypeStruct((rows, D_MODEL, S5_CHUNK), F32),
        scratch_shapes=[pltpu.VMEM((S5_COLS, S5_COLS), BF16),
                        pltpu.VMEM((S5_COLS, S5_COLS), BF16),
                        pltpu.VMEM((rows, S5_COLS), BF16),
                        pltpu.VMEM((2, rows, 2 * S5_STATE), F32),
                        pltpu.VMEM((rows, 2 * S5_STATE), F32)],
        compiler_params=_params(1),
        name="s5_core",
    )(u3d, resp, resp, w, z, a)


def _gelu_tanh(x):
    return 0.5 * x * (1.0 + jnp.tanh(math.sqrt(2.0 / math.pi) * (x + 0.044715 * (x * x * x))))


def _s5_post_kernel(x_ref, y_ref, g_ref, d_ref, w_ref, o_ref, act_ref):
    per_sub = TOKEN_TILE // S5_CHUNK
    for sub in range(S5_TILE_CHUNKS // per_sub):
        tok = slice(sub * TOKEN_TILE, (sub + 1) * TOKEN_TILE)
        x = x_ref[0, tok, :]
        skip = d_ref[...] * _rmsnorm(x, g_ref[...])
        for c in range(per_sub):
            rows = slice(sub * TOKEN_TILE + c * S5_CHUNK, sub * TOKEN_TILE + (c + 1) * S5_CHUNK)
            act_ref[rows, :] = _gelu_tanh(y_ref[sub * per_sub + c].T
                                          + skip[c * S5_CHUNK:(c + 1) * S5_CHUNK, :]).astype(BF16)
        z = _dot(act_ref[tok, :], w_ref[...])
        o_ref[0, tok, :] = x + z[:, :D_MODEL] * jax.nn.sigmoid(z[:, D_MODEL:])


def _s5_post(x3d, y3d, g, d_skip, w_glu):
    bsz, seq, _ = x3d.shape
    tile = S5_TILE_CHUNKS * S5_CHUNK
    steps = seq // tile
    row = pl.BlockSpec((1, tile, D_MODEL), lambda b, i: (b, i, 0))
    return pl.pallas_call(
        _s5_post_kernel,
        grid=(bsz, steps),
        in_specs=[row,
                  pl.BlockSpec((S5_TILE_CHUNKS, D_MODEL, S5_CHUNK), lambda b, i: (b * steps + i, 0, 0)),
                  _resident((1, D_MODEL)), _resident((1, D_MODEL)),
                  _resident((D_MODEL, 2 * D_MODEL))],
        out_specs=row,
        out_shape=jax.ShapeDtypeStruct(x3d.shape, F32),
        scratch_shapes=[pltpu.VMEM((tile, D_MODEL), BF16)],
        compiler_params=_params(2),
        name="s5_post",
    )(x3d, y3d, g, d_skip, w_glu)


def kernel(x, mem, positions, norm_gains, mem_norm, final_norm, ffn1_w_in, ffn1_w_out, ffn2_w_in, ffn2_w_out, s5_lam_re, s5_lam_im, s5_log_dt, s5_b_re, s5_b_im, s5_c_re, s5_c_im, s5_d, s5_w_glu, ret_w_in, ret_w_out, xattn_w_q, xattn_w_kv, xattn_w_o):
    bsz, seq, _ = x.shape
    tokens = bsz * seq
    gains = norm_gains.astype(F32)
    gf = final_norm.astype(F32).reshape(1, D_MODEL)

    kv = _memory_kv(mem.reshape(bsz * N_MEM, D_MODEL), mem_norm.astype(F32).reshape(1, D_MODEL),
                    xattn_w_kv.astype(BF16))
    kv = kv.reshape(DEPTH, bsz, N_MEM, 2 * D_MODEL)
    pos_lanes = jnp.broadcast_to(positions.astype(F32).reshape(tokens, 1), (tokens, ROPE_HALF))
    inv_freq = (1.0 / (ROPE_BASE ** jnp.linspace(0.0, 1.0, ROPE_HALF, dtype=F32))).reshape(1, ROPE_HALF)
    d_mat, xi, zeta = _ret_decay_tables()
    rope_tables = None

    for i in range(DEPTH):
        g = [gains[i, j].reshape(1, D_MODEL) for j in range(4)]
        ffn1 = functools.partial(_ffn, x.reshape(tokens, D_MODEL), g[0], ffn1_w_in[i].astype(BF16),
                                 ffn1_w_out[i].astype(BF16))
        j = i // 2
        if i % 2 == 0:
            resp, w, z, a = _s5_operators(s5_lam_re[j], s5_lam_im[j], s5_log_dt[j], s5_b_re[j],
                                          s5_b_im[j], s5_c_re[j], s5_c_im[j])
            x2d, u = ffn1(g[1], "s5")
            x = x2d.reshape(bsz, seq, D_MODEL)
            y = _s5_core(u, resp, w, z, a, bsz)
            x = _s5_post(x, y, g[1], s5_d[j].astype(F32).reshape(1, D_MODEL),
                         s5_w_glu[j].astype(BF16))
        else:
            x2d, = ffn1(gf)
            w_in = ret_w_in[j].astype(BF16)
            if rope_tables is None:
                *heads, cos, sin = _ret_in(x2d, g[1], (pos_lanes, inv_freq), zeta, w_in, True)
                rope_tables = (cos, sin)
            else:
                heads = _ret_in(x2d, g[1], rope_tables, zeta, w_in, False)
            q, k, kz, v, sg = [t.reshape(bsz, seq, t.shape[-1]) for t in heads]
            x = _ret_core(x2d.reshape(bsz, seq, D_MODEL), q, k, kz, v, sg, d_mat, xi,
                          ret_w_out[j].astype(BF16))
        x = _xattn(x, g[2], xattn_w_q[i].astype(BF16), kv[i], xattn_w_o[i].astype(BF16))
        x, = _ffn(x.reshape(tokens, D_MODEL), g[3], ffn2_w_in[i].astype(BF16),
                  ffn2_w_out[i].astype(BF16), gf, "final" if i == DEPTH - 1 else None)
        x = x.reshape(bsz, seq, D_MODEL)
    return x
```

```python
import functools
import math

import jax
import jax.numpy as jnp
from jax import lax
from jax.experimental import pallas as pl
from jax.experimental.pallas import tpu as pltpu

F32 = jnp.float32
BF16 = jnp.bfloat16

D_MODEL = 1024
DEPTH = 4
N_MEM = 256
D_FF = 2816
MACARON_WEIGHT = 0.5
EPS = 1e-6

S5_GROUP = 16
S5_GROUPS = D_MODEL // S5_GROUP
S5_STATE = 64
S5_CHUNK = 128
S5_COLS = S5_GROUP * S5_CHUNK
S5_TILE_CHUNKS = 8

RET_HEADS = 4
RET_QK_DIM = D_MODEL // RET_HEADS
RET_V_DIM = 2 * RET_QK_DIM
RET_VALUE_WIDTH = RET_HEADS * RET_V_DIM
RET_IN_WIDTH = 2 * D_MODEL + 2 * RET_VALUE_WIDTH
RET_CHUNK = 256
ROPE_BASE = 10000.0
ROPE_HALF = RET_QK_DIM // 2

XATTN_HEADS = 4
XATTN_DIM = D_MODEL // XATTN_HEADS

TOKEN_TILE = 512
XATTN_TILE = 1024
RET_IN_TILE = 1024
VMEM_LIMIT = 56 * 1024 * 1024


def _params(n_axes):
    return pltpu.CompilerParams(dimension_semantics=("arbitrary",) * n_axes,
                                vmem_limit_bytes=VMEM_LIMIT)


def _resident(shape):
    zeros = (0,) * len(shape)
    return pl.BlockSpec(shape, lambda *_: zeros, pipeline_mode=pl.Buffered(1))


def _rmsnorm(x, g):
    return x * lax.rsqrt(jnp.mean(x * x, axis=-1, keepdims=True) + EPS) * g


def _dot(a, b):
    return jnp.dot(a, b, preferred_element_type=F32)


def _ffn_kernel(x_ref, g_ref, w_in_ref, w_out_ref, gn_ref, o_ref, *extra_refs, extra):
    x = x_ref[...]
    xn = _rmsnorm(x, g_ref[...]).astype(BF16)
    h = _dot(xn, w_in_ref[...])
    a = h[:, :D_FF]
    act = (a * jax.nn.sigmoid(a) * h[:, D_FF:]).astype(BF16)
    y = x + MACARON_WEIGHT * _dot(act, w_out_ref[...])
    if extra == "final":
        y = _rmsnorm(y, gn_ref[...])
    o_ref[...] = y
    if extra == "s5":
        u_ref, = extra_refs
        yn = _rmsnorm(y, gn_ref[...])
        for c in range(TOKEN_TILE // S5_CHUNK):
            u_ref[c] = yn[c * S5_CHUNK:(c + 1) * S5_CHUNK, :].T


def _ffn(x2d, g, w_in, w_out, gn, extra=None):
    tokens = x2d.shape[0]
    row = pl.BlockSpec((TOKEN_TILE, D_MODEL), lambda i: (i, 0))
    out_specs = [row]
    out_shape = [jax.ShapeDtypeStruct((tokens, D_MODEL), F32)]
    if extra == "s5":
        per_tile = TOKEN_TILE // S5_CHUNK
        out_specs.append(pl.BlockSpec((per_tile, D_MODEL, S5_CHUNK), lambda i: (i, 0, 0)))
        out_shape.append(jax.ShapeDtypeStruct((tokens // S5_CHUNK, D_MODEL, S5_CHUNK), F32))
    return pl.pallas_call(
        functools.partial(_ffn_kernel, extra=extra),
        grid=(tokens // TOKEN_TILE,),
        in_specs=[row, _resident((1, D_MODEL)), _resident((D_MODEL, 2 * D_FF)),
                  _resident((D_FF, D_MODEL)), _resident((1, D_MODEL))],
        out_specs=out_specs,
        out_shape=out_shape,
        compiler_params=_params(1),
        name="ffn",
    )(x2d, g, w_in, w_out, gn)


def _kv_kernel(mem_ref, g_ref, w_ref, o_ref):
    mn = _rmsnorm(mem_ref[...], g_ref[...]).astype(BF16)
    o_ref[0] = _dot(mn, w_ref[0]).astype(BF16)


def _memory_kv(mem2d, g, w_kv):
    rows = mem2d.shape[0]
    half = rows // 2
    return pl.pallas_call(
        _kv_kernel,
        grid=(DEPTH, 2),
        in_specs=[pl.BlockSpec((half, D_MODEL), lambda l, r: (r, 0)),
                  _resident((1, D_MODEL)),
                  pl.BlockSpec((1, D_MODEL, 2 * D_MODEL), lambda l, r: (l, 0, 0))],
        out_specs=pl.BlockSpec((1, half, 2 * D_MODEL), lambda l, r: (l, r, 0)),
        out_shape=jax.ShapeDtypeStruct((DEPTH, rows, 2 * D_MODEL), BF16),
        compiler_params=_params(2),
        name="memory_kv",
    )(mem2d, g, w_kv)


def _xattn_kernel(x_ref, g_ref, wq_ref, k_ref, v_ref, wo_ref, o_ref):
    for sub in range(XATTN_TILE // TOKEN_TILE):
        rows = slice(sub * TOKEN_TILE, (sub + 1) * TOKEN_TILE)
        x = x_ref[0, rows, :]
        xn = _rmsnorm(x, g_ref[...]).astype(BF16)
        q = (_dot(xn, wq_ref[...]) * (XATTN_DIM ** -0.5)).astype(BF16)
        heads = []
        for h in range(XATTN_HEADS):
            sl = slice(h * XATTN_DIM, (h + 1) * XATTN_DIM)
            s = lax.dot_general(q[:, sl], k_ref[0, :, sl], (((1,), (1,)), ((), ())),
                                preferred_element_type=F32)
            e = jnp.exp(s - jnp.max(s, axis=-1, keepdims=True))
            p = e / jnp.sum(e, axis=-1, keepdims=True)
            heads.append(_dot(p.astype(BF16), v_ref[0, :, sl]).astype(BF16))
        o = jnp.concatenate(heads, axis=-1)
        o_ref[0, rows, :] = x + _dot(o, wo_ref[...])


def _xattn(x3d, g, w_q, kv, w_o):
    bsz, seq, _ = x3d.shape
    row = pl.BlockSpec((1, XATTN_TILE, D_MODEL), lambda b, i: (b, i, 0))
    return pl.pallas_call(
        _xattn_kernel,
        grid=(bsz, seq // XATTN_TILE),
        in_specs=[row, _resident((1, D_MODEL)), _resident((D_MODEL, D_MODEL)),
                  pl.BlockSpec((1, N_MEM, D_MODEL), lambda b, i: (b, 0, 0)),
                  pl.BlockSpec((1, N_MEM, D_MODEL), lambda b, i: (b, 0, 1)),
                  _resident((D_MODEL, D_MODEL))],
        out_specs=row,
        out_shape=jax.ShapeDtypeStruct(x3d.shape, F32),
        compiler_params=_params(2),
        name="xattn",
    )(x3d, g, w_q, kv, kv, w_o)


def _ret_decay_tables():
    c = RET_CHUNK
    log_gamma = jnp.log(1.0 - jnp.exp2(-5.0 - jnp.arange(RET_HEADS, dtype=F32)))
    idx = jnp.arange(c)
    diff = idx[:, None] - idx[None, :]
    d_mat = jnp.where(diff >= 0, jnp.exp(log_gamma[:, None, None]
                                         * jnp.maximum(diff, 0).astype(F32)), 0.0)
    xi = jnp.exp(log_gamma[:, None] * (idx + 1).astype(F32))
    zeta = jnp.exp(log_gamma[:, None] * (c - 1 - idx).astype(F32))
    xi = jnp.broadcast_to(xi[:, :, None], (RET_HEADS, c, 128))
    zeta = jnp.broadcast_to(zeta.T[:, :, None], (c, RET_HEADS, RET_QK_DIM)).reshape(c, D_MODEL)
    return d_mat, xi, zeta


def _ret_in_kernel(x_ref, g_ref, rope_a_ref, rope_b_ref, zeta_ref, w_ref,
                   q_ref, k_ref, kz_ref, v_ref, sg_ref, *rope_out_refs, make_rope):
    if make_rope:
        cos_ref, sin_ref = rope_out_refs
        ang = rope_a_ref[...] * rope_b_ref[...]
        cos_ref[...] = jnp.cos(ang)
        sin_ref[...] = jnp.sin(ang)
    else:
        cos_ref, sin_ref = rope_a_ref, rope_b_ref
    k_scale = RET_QK_DIM ** -0.5
    for sub in range(RET_IN_TILE // TOKEN_TILE):
        base = sub * TOKEN_TILE
        tok = slice(base, base + TOKEN_TILE)
        xn = _rmsnorm(x_ref[tok, :], g_ref[...]).astype(BF16)
        h = _dot(xn, w_ref[...])
        cos = cos_ref[tok, :]
        sin = sin_ref[tok, :]
        for hd in range(RET_HEADS):
            lo = hd * RET_QK_DIM
            mid = lo + ROPE_HALF
            hi = lo + RET_QK_DIM
            q1, q2 = h[:, lo:mid], h[:, mid:hi]
            q_ref[tok, lo:mid] = (q1 * cos - q2 * sin).astype(BF16)
            q_ref[tok, mid:hi] = (q1 * sin + q2 * cos).astype(BF16)
            k1 = h[:, D_MODEL + lo:D_MODEL + mid] * k_scale
            k2 = h[:, D_MODEL + mid:D_MODEL + hi] * k_scale
            for dst, kr in ((slice(lo, mid), k1 * cos - k2 * sin),
                            (slice(mid, hi), k1 * sin + k2 * cos)):
                k_ref[tok, dst] = kr.astype(BF16)
                for ci in range(TOKEN_TILE // RET_CHUNK):
                    rows = slice(ci * RET_CHUNK, (ci + 1) * RET_CHUNK)
                    kz_ref[base + ci * RET_CHUNK:base + (ci + 1) * RET_CHUNK, dst] = (
                        kr[rows, :] * zeta_ref[:, dst]).astype(BF16)
        v_ref[tok, :] = h[:, 2 * D_MODEL:2 * D_MODEL + RET_VALUE_WIDTH].astype(BF16)
        gate = h[:, 2 * D_MODEL + RET_VALUE_WIDTH:]
        sg_ref[tok, :] = (gate * jax.nn.sigmoid(gate)).astype(BF16)


def _ret_in(x2d, g, rope, zeta, w_in, make_rope):
    tokens = x2d.shape[0]

    def row(width):
        return pl.BlockSpec((RET_IN_TILE, width), lambda i: (i, 0))

    def out(width, dtype=BF16):
        return jax.ShapeDtypeStruct((tokens, width), dtype)

    rope_b = _resident((1, ROPE_HALF)) if make_rope else row(ROPE_HALF)
    rope_out = 2 if make_rope else 0
    return pl.pallas_call(
        functools.partial(_ret_in_kernel, make_rope=make_rope),
        grid=(tokens // RET_IN_TILE,),
        in_specs=[row(D_MODEL), _resident((1, D_MODEL)), row(ROPE_HALF), rope_b,
                  _resident((RET_CHUNK, D_MODEL)), _resident((D_MODEL, RET_IN_WIDTH))],
        out_specs=[row(D_MODEL), row(D_MODEL), row(D_MODEL), row(RET_VALUE_WIDTH),
                   row(RET_VALUE_WIDTH)] + [row(ROPE_HALF)] * rope_out,
        out_shape=[out(D_MODEL), out(D_MODEL), out(D_MODEL), out(RET_VALUE_WIDTH),
                   out(RET_VALUE_WIDTH)] + [out(ROPE_HALF, F32)] * rope_out,
        compiler_params=_params(1),
        name="ret_in",
    )(x2d, g, *rope, zeta, w_in)


def _ret_core_kernel(x_ref, q_ref, k_ref, kz_ref, v_ref, sg_ref, d_ref, xi_ref, w_out_ref, o_ref,
                     state_ref, y_ref):
    @pl.when(pl.program_id(1) == 0)
    def _():
        state_ref[...] = jnp.zeros_like(state_ref)

    c = RET_CHUNK
    for hd in range(RET_HEADS):
        gamma_chunk = (1.0 - 2.0 ** (-5 - hd)) ** c
        xi = jnp.concatenate([xi_ref[hd]] * (RET_V_DIM // 128), axis=-1)
        qk = slice(hd * RET_QK_DIM, (hd + 1) * RET_QK_DIM)
        vv = slice(hd * RET_V_DIM, (hd + 1) * RET_V_DIM)
        for ci in range(TOKEN_TILE // c):
            rows = slice(ci * c, (ci + 1) * c)
            qc = q_ref[0, rows, qk]
            vc = v_ref[0, rows, vv]
            state = state_ref[hd]
            inner = lax.dot_general(qc, k_ref[0, rows, qk], (((1,), (1,)), ((), ())),
                                    preferred_element_type=F32) * d_ref[hd]
            o = _dot(inner.astype(BF16), vc) + _dot(qc, state.astype(BF16)) * xi
            state_ref[hd] = state * gamma_chunk + lax.dot_general(
                kz_ref[0, rows, qk], vc, (((0,), (0,)), ((), ())), preferred_element_type=F32)
            mu = jnp.mean(o, axis=-1, keepdims=True)
            dev = o - mu
            var = jnp.mean(dev * dev, axis=-1, keepdims=True)
            y = sg_ref[0, rows, vv].astype(F32) * (dev * lax.rsqrt(var + EPS))
            y_ref[rows, vv] = y.astype(BF16)
    o_ref[0] = x_ref[0] + _dot(y_ref[...], w_out_ref[...])


def _ret_core(x3d, q, k, kz, v, sg, d_mat, xi, w_out):
    bsz, seq, _ = x3d.shape

    def row(width):
        return pl.BlockSpec((1, TOKEN_TILE, width), lambda b, i: (b, i, 0))

    return pl.pallas_call(
        _ret_core_kernel,
        grid=(bsz, seq // TOKEN_TILE),
        in_specs=[row(D_MODEL), row(D_MODEL), row(D_MODEL), row(D_MODEL), row(RET_VALUE_WIDTH),
                  row(RET_VALUE_WIDTH), _resident((RET_HEADS, RET_CHUNK, RET_CHUNK)),
                  _resident((RET_HEADS, RET_CHUNK, 128)), _resident((RET_VALUE_WIDTH, D_MODEL))],
        out_specs=row(D_MODEL),
        out_shape=jax.ShapeDtypeStruct(x3d.shape, F32),
        scratch_shapes=[pltpu.VMEM((RET_HEADS, RET_QK_DIM, RET_V_DIM), F32),
                        pltpu.VMEM((TOKEN_TILE, RET_VALUE_WIDTH), BF16)],
        compiler_params=_params(2),
        name="ret_core",
    )(x3d, q, k, kz, v, sg, d_mat, xi, w_out)


def _s5_operator_kernel(rows_ref, cols_ref, b_ref, c_ref, ct_ref, w_ref, z_ref, r_ref, a_ref, m_ref):
    n = S5_CHUNK
    lane = lax.broadcasted_iota(jnp.int32, (1, 2 * S5_STATE), 1)
    sign = jnp.where(lane < S5_STATE, -1.0, 1.0)
    lr, li = rows_ref[0, 0:1, :], rows_ref[0, 1:2, :]
    dt = jnp.exp(rows_ref[0, 2:3, :])
    mag = jnp.exp(lr * dt)
    ar = mag * jnp.cos(li * dt)
    ai = mag * jnp.sin(li * dt)
    den = lr * lr + li * li
    nr = ar - 1.0
    zr = (nr * lr + ai * li) / den
    zi = (ai * lr - nr * li) / den
    bb_ri = zr * b_ref[0, 0] + zi * b_ref[0, 1]
    bb_ir = zr * b_ref[0, 2] + zi * b_ref[0, 3]
    bb_rr = jnp.where(lane < S5_STATE, bb_ri, bb_ir)
    bb_ii = jnp.where(lane < S5_STATE, bb_ir, bb_ri)

    full_mag = jnp.exp(lr * dt * n)
    alr = full_mag * jnp.cos(li * dt * n)
    ali = full_mag * jnp.sin(li * dt * n) * sign
    row = lax.broadcasted_iota(jnp.int32, (8, 2 * S5_STATE), 0)
    a_ref[0] = jnp.where(row == 0, alr, jnp.where(row == 1, ali, jnp.where(row == 2, -ali, 0.0)))

    back = (n - 1 - lax.broadcasted_iota(jnp.int32, (n, 2 * S5_STATE), 0)).astype(F32)
    back_mag = jnp.exp(lr * dt * back)
    pw_r = back_mag * jnp.cos(li * dt * back)
    pw_i = back_mag * jnp.sin(li * dt * back)
    for hi in range(S5_GROUP):
        sl = slice(hi, hi + 1)
        rows = slice(hi * n, (hi + 1) * n)
        w_ref[0, rows, :2 * S5_STATE] = (pw_r * bb_ri[sl] + pw_i * (bb_ir[sl] * sign)).astype(BF16)
        w_ref[0, rows, 2 * S5_STATE:] = (pw_r * bb_ir[sl] - pw_i * (bb_ri[sl] * sign)).astype(BF16)
        m_ref[hi * S5_GROUP:(hi + 1) * S5_GROUP, :] = bb_rr[sl] * c_ref[0, 0] + bb_ii[sl] * c_ref[0, 1]

    fwd = (lax.broadcasted_iota(jnp.int32, (2 * S5_STATE, n), 1) + 1).astype(F32)
    fwd_mag = jnp.exp(cols_ref[0, 0] * dt * fwd)
    p1_r = fwd_mag * jnp.cos(cols_ref[0, 1] * dt * fwd)
    p1_i = fwd_mag * jnp.sin(cols_ref[0, 1] * dt * fwd)
    for ho in range(S5_GROUP):
        z_ref[0, :, ho * n:(ho + 1) * n] = (ct_ref[0, 0][:, ho:ho + 1] * p1_r
                                            + ct_ref[0, 1][:, ho:ho + 1] * p1_i).astype(BF16)
    first = lax.broadcasted_iota(jnp.int32, (2 * S5_STATE, n), 1) == 0
    p0_r = jnp.where(first, 1.0, pltpu.roll(p1_r, 1, 1))
    p0_i = jnp.where(first, 0.0, pltpu.roll(p1_i, 1, 1))
    top = lax.broadcasted_iota(jnp.int32, (2 * S5_STATE, n), 0) < S5_STATE
    r_ref[0] = jnp.dot(m_ref[...], jnp.where(top, p0_r, p0_i), preferred_element_type=F32,
                       precision=lax.Precision.HIGHEST)


def _s5_operators(lam_re, lam_im, log_dt, b_re, b_im, c_re, c_im):
    def pair(u, v):
        return jnp.concatenate([u, v], axis=-1)

    lr, li = lam_re.astype(F32), lam_im.astype(F32)
    lam = jnp.stack([pair(lr, lr), pair(li, li)], axis=1)
    ldt = jnp.broadcast_to(log_dt.astype(F32)[:, None, None], (S5_GROUPS, 1, 2 * S5_STATE))
    rows = jnp.concatenate([lam, ldt, jnp.zeros((S5_GROUPS, 5, 2 * S5_STATE), F32)], axis=1)
    cols = jnp.broadcast_to(lam[..., None], (S5_GROUPS, 2, 2 * S5_STATE, S5_CHUNK))
    br, bi = b_re.astype(F32).transpose(0, 2, 1), b_im.astype(F32).transpose(0, 2, 1)
    b = jnp.stack([pair(br, bi), pair(-bi, br), pair(bi, br), pair(br, -bi)], axis=1)
    cr, ci = c_re.astype(F32), c_im.astype(F32)
    c = jnp.stack([pair(cr, -ci), pair(-ci, -cr)], axis=1)
    ct = c.transpose(0, 1, 3, 2)

    def per_group(*shape):
        return pl.BlockSpec((1,) + shape, lambda g: (g,) + (0,) * len(shape))

    w, z, resp, a = pl.pallas_call(
        _s5_operator_kernel,
        grid=(S5_GROUPS,),
        in_specs=[per_group(8, 2 * S5_STATE), per_group(2, 2 * S5_STATE, S5_CHUNK),
                  per_group(4, S5_GROUP, 2 * S5_STATE), per_group(2, S5_GROUP, 2 * S5_STATE),
                  per_group(2, 2 * S5_STATE, S5_GROUP)],
        out_specs=[per_group(S5_COLS, 4 * S5_STATE), per_group(2 * S5_STATE, S5_COLS),
                   per_group(S5_GROUP * S5_GROUP, S5_CHUNK), per_group(8, 2 * S5_STATE)],
        out_shape=[jax.ShapeDtypeStruct((S5_GROUPS, S5_COLS, 4 * S5_STATE), BF16),
                   jax.ShapeDtypeStruct((S5_GROUPS, 2 * S5_STATE, S5_COLS), BF16),
                   jax.ShapeDtypeStruct((S5_GROUPS, S5_GROUP * S5_GROUP, S5_CHUNK), F32),
                   jax.ShapeDtypeStruct((S5_GROUPS, 8, 2 * S5_STATE), F32)],
        scratch_shapes=[pltpu.VMEM((S5_GROUP * S5_GROUP, 2 * S5_STATE), F32)],
        compiler_params=_params(1),
        name="s5_operators",
    )(rows, cols, b, c, ct)
    return resp, w, z, a


def _channel_rows(ref, ch):
    rows, channels, width = ref.shape
    return ref.reshape(rows * channels, width).at[pl.ds(ch, rows, stride=channels), :]


def _toeplitz_block(r_ref, row, causal):
    n = S5_CHUNK
    lags = jnp.broadcast_to(r_ref[0, pl.ds(row, 1), :], (n, n))
    blk = pltpu.roll(lags, 0, 1, stride=1, stride_axis=0)
    return jnp.where(causal, blk, 0.0).astype(BF16)


def _s5_core_kernel(u_ref, r0_ref, r_next_ref, w_ref, z_ref, a_ref, y_ref, t0_ref, t1_ref, us_ref,
                    v_ref, xin_ref, *, bsz, n_chunks):
    n = S5_CHUNK
    g = pl.program_id(0)
    causal = (lax.broadcasted_iota(jnp.int32, (n, n), 1)
              >= lax.broadcasted_iota(jnp.int32, (n, n), 0))

    @pl.when(g == 0)
    def _():
        def build_rows(hi, carry):
            for ho in range(S5_GROUP):
                t0_ref[pl.ds(pl.multiple_of(hi * n, n), n), ho * n:(ho + 1) * n] = (
                    _toeplitz_block(r0_ref, hi * S5_GROUP + ho, causal))
            return carry

        lax.fori_loop(0, S5_GROUP, build_rows, 0)

    @pl.when(g % 2 == 0)
    def _():
        _s5_group(u_ref, r_next_ref, w_ref, z_ref, a_ref, y_ref, t0_ref, t1_ref, us_ref, v_ref,
                  xin_ref, causal, bsz, n_chunks)

    @pl.when(g % 2 == 1)
    def _():
        _s5_group(u_ref, r_next_ref, w_ref, z_ref, a_ref, y_ref, t1_ref, t0_ref, us_ref, v_ref,
                  xin_ref, causal, bsz, n_chunks)


def _s5_group(u_ref, r_next_ref, w_ref, z_ref, a_ref, y_ref, t_ref, t_next_ref, us_ref, v_ref,
              xin_ref, causal, bsz, n_chunks):
    n = S5_CHUNK
    for hi in range(S5_GROUP):
        for ho in range(S5_GROUP):
            t_next_ref[hi * n:(hi + 1) * n, ho * n:(ho + 1) * n] = (
                _toeplitz_block(r_next_ref, hi * S5_GROUP + ho, causal))

    for hi in range(S5_GROUP):
        us_ref[:, hi * n:(hi + 1) * n] = _channel_rows(u_ref, hi)[...].astype(BF16)
    u = us_ref[...]
    half = 2 * S5_STATE
    v = _dot(u, w_ref[0])
    v_ref[0] = v[:, :half]
    v_ref[1] = v[:, half:]
    a_same = a_ref[0, 0:1, :]
    a_cross_p = a_ref[0, 1:2, :]
    a_cross_q = a_ref[0, 2:3, :]
    p = jnp.zeros((bsz, half), F32)
    q = jnp.zeros((bsz, half), F32)
    for c in range(n_chunks):
        xin_ref[pl.ds(c, bsz, stride=n_chunks), :] = p
        p, q = (a_same * p + a_cross_p * q + v_ref[0, pl.ds(c, bsz, stride=n_chunks), :],
                a_same * q + a_cross_q * p + v_ref[1, pl.ds(c, bsz, stride=n_chunks), :])
    y = _dot(u, t_ref[...]) + _dot(xin_ref[...].astype(BF16), z_ref[0])
    for ho in range(S5_GROUP):
        _channel_rows(y_ref, ho)[...] = y[:, ho * n:(ho + 1) * n]


def _s5_core(u3d, resp, w, z, a, bsz):
    rows = u3d.shape[0]
    chunk_rows = pl.BlockSpec((rows, S5_GROUP, S5_CHUNK), lambda g: (0, g, 0))
    return pl.pallas_call(
        functools.partial(_s5_core_kernel, bsz=bsz, n_chunks=rows // bsz),
        grid=(S5_GROUPS,),
        in_specs=[chunk_rows,
                  pl.BlockSpec((1, S5_GROUP * S5_GROUP, S5_CHUNK), lambda g: (0, 0, 0)),
                  pl.BlockSpec((1, S5_GROUP * S5_GROUP, S5_CHUNK),
                               lambda g: (jnp.minimum(g + 1, S5_GROUPS - 1), 0, 0)),
                  pl.BlockSpec((1, S5_COLS, 4 * S5_STATE), lambda g: (g, 0, 0)),
                  pl.BlockSpec((1, 2 * S5_STATE, S5_COLS), lambda g: (g, 0, 0)),
                  pl.BlockSpec((1, 8, 2 * S5_STATE), lambda g: (g, 0, 0))],
        out_specs=chunk_rows,
        out_shape=jax.ShapeDtypeStruct((rows, D_MODEL, S5_CHUNK), F32),
        scratch_shapes=[pltpu.VMEM((S5_COLS, S5_COLS), BF16),
                        pltpu.VMEM((S5_COLS, S5_COLS), BF16),
                        pltpu.VMEM((rows, S5_COLS), BF16),
                        pltpu.VMEM((2, rows, 2 * S5_STATE), F32),
                        pltpu.VMEM((rows, 2 * S5_STATE), F32)],
        compiler_params=_params(1),
        name="s5_core",
    )(u3d, resp, resp, w, z, a)


def _gelu_tanh(x):
    return 0.5 * x * (1.0 + jnp.tanh(math.sqrt(2.0 / math.pi) * (x + 0.044715 * (x * x * x))))


def _s5_post_kernel(x_ref, y_ref, g_ref, d_ref, w_ref, o_ref, act_ref):
    per_sub = TOKEN_TILE // S5_CHUNK
    for sub in range(S5_TILE_CHUNKS // per_sub):
        tok = slice(sub * TOKEN_TILE, (sub + 1) * TOKEN_TILE)
        x = x_ref[0, tok, :]
        skip = d_ref[...] * _rmsnorm(x, g_ref[...])
        for c in range(per_sub):
            rows = slice(sub * TOKEN_TILE + c * S5_CHUNK, sub * TOKEN_TILE + (c + 1) * S5_CHUNK)
            act_ref[rows, :] = _gelu_tanh(y_ref[sub * per_sub + c].T
                                          + skip[c * S5_CHUNK:(c + 1) * S5_CHUNK, :]).astype(BF16)
        z = _dot(act_ref[tok, :], w_ref[...])
        o_ref[0, tok, :] = x + z[:, :D_MODEL] * jax.nn.sigmoid(z[:, D_MODEL:])


def _s5_post(x3d, y3d, g, d_skip, w_glu):
    bsz, seq, _ = x3d.shape
    tile = S5_TILE_CHUNKS * S5_CHUNK
    steps = seq // tile
    row = pl.BlockSpec((1, tile, D_MODEL), lambda b, i: (b, i, 0))
    return pl.pallas_call(
        _s5_post_kernel,
        grid=(bsz, steps),
        in_specs=[row,
                  pl.BlockSpec((S5_TILE_CHUNKS, D_MODEL, S5_CHUNK), lambda b, i: (b * steps + i, 0, 0)),
                  _resident((1, D_MODEL)), _resident((1, D_MODEL)),
                  _resident((D_MODEL, 2 * D_MODEL))],
        out_specs=row,
        out_shape=jax.ShapeDtypeStruct(x3d.shape, F32),
        scratch_shapes=[pltpu.VMEM((tile, D_MODEL), BF16)],
        compiler_params=_params(2),
        name="s5_post",
    )(x3d, y3d, g, d_skip, w_glu)


def kernel(x, mem, positions, norm_gains, mem_norm, final_norm, ffn1_w_in, ffn1_w_out, ffn2_w_in, ffn2_w_out, s5_lam_re, s5_lam_im, s5_log_dt, s5_b_re, s5_b_im, s5_c_re, s5_c_im, s5_d, s5_w_glu, ret_w_in, ret_w_out, xattn_w_q, xattn_w_kv, xattn_w_o):
    bsz, seq, _ = x.shape
    tokens = bsz * seq
    gains = norm_gains.astype(F32)
    gf = final_norm.astype(F32).reshape(1, D_MODEL)

    kv = _memory_kv(mem.reshape(bsz * N_MEM, D_MODEL), mem_norm.astype(F32).reshape(1, D_MODEL),
                    xattn_w_kv.astype(BF16))
    kv = kv.reshape(DEPTH, bsz, N_MEM, 2 * D_MODEL)
    pos_lanes = jnp.broadcast_to(positions.astype(F32).reshape(tokens, 1), (tokens, ROPE_HALF))
    inv_freq = (1.0 / (ROPE_BASE ** jnp.linspace(0.0, 1.0, ROPE_HALF, dtype=F32))).reshape(1, ROPE_HALF)
    d_mat, xi, zeta = _ret_decay_tables()
    rope_tables = None

    for i in range(DEPTH):
        g = [gains[i, j].reshape(1, D_MODEL) for j in range(4)]
        ffn1 = functools.partial(_ffn, x.reshape(tokens, D_MODEL), g[0], ffn1_w_in[i].astype(BF16),
                                 ffn1_w_out[i].astype(BF16))
        j = i // 2
        if i % 2 == 0:
            resp, w, z, a = _s5_operators(s5_lam_re[j], s5_lam_im[j], s5_log_dt[j], s5_b_re[j],
                                          s5_b_im[j], s5_c_re[j], s5_c_im[j])
            x2d, u = ffn1(g[1], "s5")
            x = x2d.reshape(bsz, seq, D_MODEL)
            y = _s5_core(u, resp, w, z, a, bsz)
            x = _s5_post(x, y, g[1], s5_d[j].astype(F32).reshape(1, D_MODEL),
                         s5_w_glu[j].astype(BF16))
        else:
            x2d, = ffn1(gf)
            w_in = ret_w_in[j].astype(BF16)
            if rope_tables is None:
                *heads, cos, sin = _ret_in(x2d, g[1], (pos_lanes, inv_freq), zeta, w_in, True)
                rope_tables = (cos, sin)
            else:
                heads = _ret_in(x2d, g[1], rope_tables, zeta, w_in, False)
            q, k, kz, v, sg = [t.reshape(bsz, seq, t.shape[-1]) for t in heads]
            x = _ret_core(x2d.reshape(bsz, seq, D_MODEL), q, k, kz, v, sg, d_mat, xi,
                          ret_w_out[j].astype(BF16))
        x = _xattn(x, g[2], xattn_w_q[i].astype(BF16), kv[i], xattn_w_o[i].astype(BF16))
        x, = _ffn(x.reshape(tokens, D_MODEL), g[3], ffn2_w_in[i].astype(BF16),
                  ffn2_w_out[i].astype(BF16), gf, "final" if i == DEPTH - 1 else None)
        x = x.reshape(bsz, seq, D_MODEL)
    return x
```

```python
import functools
import math

import jax
import jax.numpy as jnp
from jax import lax
from jax.experimental import pallas as pl
from jax.experimental.pallas import tpu as pltpu

F32 = jnp.float32
BF16 = jnp.bfloat16

D_MODEL = 1024
DEPTH = 4
N_MEM = 256
D_FF = 2816
MACARON_WEIGHT = 0.5
EPS = 1e-6

S5_GROUP = 16
S5_GROUPS = D_MODEL // S5_GROUP
S5_STATE = 64
S5_CHUNK = 128
S5_COLS = S5_GROUP * S5_CHUNK
S5_TILE_CHUNKS = 8

RET_HEADS = 4
RET_QK_DIM = D_MODEL // RET_HEADS
RET_V_DIM = 2 * RET_QK_DIM
RET_VALUE_WIDTH = RET_HEADS * RET_V_DIM
RET_IN_WIDTH = 2 * D_MODEL + 2 * RET_VALUE_WIDTH
RET_CHUNK = 256
ROPE_BASE = 10000.0
ROPE_HALF = RET_QK_DIM // 2

XATTN_HEADS = 4
XATTN_DIM = D_MODEL // XATTN_HEADS

TOKEN_TILE = 512
FFN_TILE = 1024
FF_BLOCK = 256
XATTN_TILE = 1024
RET_IN_TILE = 1024
VMEM_LIMIT = 56 * 1024 * 1024


def _params(n_axes):
    return pltpu.CompilerParams(dimension_semantics=("arbitrary",) * n_axes,
                                vmem_limit_bytes=VMEM_LIMIT)


def _resident(shape):
    zeros = (0,) * len(shape)
    return pl.BlockSpec(shape, lambda *_: zeros, pipeline_mode=pl.Buffered(1))


def _layer_resident(shape, layer):
    zeros = (0,) * len(shape)
    return pl.BlockSpec((None,) + shape, lambda *_: (layer,) + zeros, pipeline_mode=pl.Buffered(1))


def _rmsnorm(x, g):
    return x * lax.rsqrt(jnp.mean(x * x, axis=-1, keepdims=True) + EPS) * g


def _dot(a, b):
    return jnp.dot(a, b, preferred_element_type=F32)


def _ffn_kernel(x_ref, g_ref, w_in_ref, w_out_ref, gn_ref, o_ref, *extra_refs, extra):
    *extra_refs, act_ref = extra_refs
    per_sub = TOKEN_TILE // S5_CHUNK
    for sub in range(FFN_TILE // TOKEN_TILE):
        tok = slice(sub * TOKEN_TILE, (sub + 1) * TOKEN_TILE)
        x = x_ref[tok, :]
        xn = _rmsnorm(x, g_ref[...]).astype(BF16)
        for j in range(D_FF // FF_BLOCK):
            cols = slice(j * FF_BLOCK, (j + 1) * FF_BLOCK)
            a = _dot(xn, w_in_ref[:, cols])
            b = _dot(xn, w_in_ref[:, D_FF + j * FF_BLOCK:D_FF + (j + 1) * FF_BLOCK])
            act_ref[tok, cols] = (a * jax.nn.sigmoid(a) * b).astype(BF16)
        y = x + MACARON_WEIGHT * _dot(act_ref[tok, :], w_out_ref[...])
        if extra == "final":
            y = _rmsnorm(y, gn_ref[...])
        o_ref[tok, :] = y
        if extra == "s5":
            u_ref, = extra_refs
            yn = _rmsnorm(y, gn_ref[...])
            for c in range(per_sub):
                u_ref[sub * per_sub + c] = yn[c * S5_CHUNK:(c + 1) * S5_CHUNK, :].T


def _ffn(x2d, g, w_in, w_out, layer, gn, extra=None):
    tokens = x2d.shape[0]
    row = pl.BlockSpec((FFN_TILE, D_MODEL), lambda i: (i, 0))
    out_specs = [row]
    out_shape = [jax.ShapeDtypeStruct((tokens, D_MODEL), F32)]
    if extra == "s5":
        per_tile = FFN_TILE // S5_CHUNK
        out_specs.append(pl.BlockSpec((per_tile, D_MODEL, S5_CHUNK), lambda i: (i, 0, 0)))
        out_shape.append(jax.ShapeDtypeStruct((tokens // S5_CHUNK, D_MODEL, S5_CHUNK), F32))
    return pl.pallas_call(
        functools.partial(_ffn_kernel, extra=extra),
        grid=(tokens // FFN_TILE,),
        in_specs=[row, _resident((1, D_MODEL)), _layer_resident((D_MODEL, 2 * D_FF), layer),
                  _layer_resident((D_FF, D_MODEL), layer), _resident((1, D_MODEL))],
        out_specs=out_specs,
        out_shape=out_shape,
        scratch_shapes=[pltpu.VMEM((FFN_TILE, D_FF), BF16)],
        compiler_params=_params(1),
        name="ffn",
    )(x2d, g, w_in, w_out, gn)


def _kv_kernel(mem_ref, g_ref, w_ref, o_ref):
    mn = _rmsnorm(mem_ref[...], g_ref[...]).astype(BF16)
    o_ref[0] = _dot(mn, w_ref[0]).astype(BF16)


def _memory_kv(mem2d, g, w_kv):
    rows = mem2d.shape[0]
    half = rows // 2
    return pl.pallas_call(
        _kv_kernel,
        grid=(DEPTH, 2),
        in_specs=[pl.BlockSpec((half, D_MODEL), lambda l, r: (r, 0)),
                  _resident((1, D_MODEL)),
                  pl.BlockSpec((1, D_MODEL, 2 * D_MODEL), lambda l, r: (l, 0, 0))],
        out_specs=pl.BlockSpec((1, half, 2 * D_MODEL), lambda l, r: (l, r, 0)),
        out_shape=jax.ShapeDtypeStruct((DEPTH, rows, 2 * D_MODEL), BF16),
        compiler_params=_params(2),
        name="memory_kv",
    )(mem2d, g, w_kv)


def _xattn_kernel(x_ref, g_ref, wq_ref, k_ref, v_ref, wo_ref, o_ref):
    for sub in range(XATTN_TILE // TOKEN_TILE):
        rows = slice(sub * TOKEN_TILE, (sub + 1) * TOKEN_TILE)
        x = x_ref[0, rows, :]
        xn = _rmsnorm(x, g_ref[...]).astype(BF16)
        q = (_dot(xn, wq_ref[...]) * (XATTN_DIM ** -0.5)).astype(BF16)
        heads = []
        for h in range(XATTN_HEADS):
            sl = slice(h * XATTN_DIM, (h + 1) * XATTN_DIM)
            s = lax.dot_general(q[:, sl], k_ref[0, :, sl], (((1,), (1,)), ((), ())),
                                preferred_element_type=F32)
            e = jnp.exp(s - jnp.max(s, axis=-1, keepdims=True))
            p = e / jnp.sum(e, axis=-1, keepdims=True)
            heads.append(_dot(p.astype(BF16), v_ref[0, :, sl]).astype(BF16))
        o = jnp.concatenate(heads, axis=-1)
        o_ref[0, rows, :] = x + _dot(o, wo_ref[...])


def _xattn(x3d, g, w_q, kv, w_o, layer):
    bsz, seq, _ = x3d.shape
    row = pl.BlockSpec((1, XATTN_TILE, D_MODEL), lambda b, i: (b, i, 0))
    return pl.pallas_call(
        _xattn_kernel,
        grid=(bsz, seq // XATTN_TILE),
        in_specs=[row, _resident((1, D_MODEL)), _layer_resident((D_MODEL, D_MODEL), layer),
                  pl.BlockSpec((None, 1, N_MEM, D_MODEL), lambda b, i: (layer, b, 0, 0)),
                  pl.BlockSpec((None, 1, N_MEM, D_MODEL), lambda b, i: (layer, b, 0, 1)),
                  _layer_resident((D_MODEL, D_MODEL), layer)],
        out_specs=row,
        out_shape=jax.ShapeDtypeStruct(x3d.shape, F32),
        compiler_params=_params(2),
        name="xattn",
    )(x3d, g, w_q, kv, kv, w_o)


def _ret_decay_tables():
    c = RET_CHUNK
    log_gamma = jnp.log(1.0 - jnp.exp2(-5.0 - jnp.arange(RET_HEADS, dtype=F32)))
    idx = jnp.arange(c)
    diff = idx[:, None] - idx[None, :]
    d_mat = jnp.where(diff >= 0, jnp.exp(log_gamma[:, None, None]
                                         * jnp.maximum(diff, 0).astype(F32)), 0.0)
    xi = jnp.exp(log_gamma[:, None] * (idx + 1).astype(F32))
    zeta = jnp.exp(log_gamma[:, None] * (c - 1 - idx).astype(F32))
    xi = jnp.broadcast_to(xi[:, :, None], (RET_HEADS, c, 128))
    zeta = jnp.broadcast_to(zeta.T[:, :, None], (c, RET_HEADS, RET_QK_DIM)).reshape(c, D_MODEL)
    return d_mat, xi, zeta


def _ret_in_kernel(x_ref, g_ref, rope_a_ref, rope_b_ref, zeta_ref, w_ref,
                   q_ref, k_ref, kz_ref, v_ref, sg_ref, *rope_out_refs, make_rope):
    if make_rope:
        cos_ref, sin_ref = rope_out_refs
        ang = rope_a_ref[...] * rope_b_ref[...]
        cos_ref[...] = jnp.cos(ang)
        sin_ref[...] = jnp.sin(ang)
    else:
        cos_ref, sin_ref = rope_a_ref, rope_b_ref
    k_scale = RET_QK_DIM ** -0.5
    for sub in range(RET_IN_TILE // TOKEN_TILE):
        base = sub * TOKEN_TILE
        tok = slice(base, base + TOKEN_TILE)
        xn = _rmsnorm(x_ref[tok, :], g_ref[...]).astype(BF16)
        h = _dot(xn, w_ref[...])
        cos = cos_ref[tok, :]
        sin = sin_ref[tok, :]
        for hd in range(RET_HEADS):
            lo = hd * RET_QK_DIM
            mid = lo + ROPE_HALF
            hi = lo + RET_QK_DIM
            q1, q2 = h[:, lo:mid], h[:, mid:hi]
            q_ref[tok, lo:mid] = (q1 * cos - q2 * sin).astype(BF16)
            q_ref[tok, mid:hi] = (q1 * sin + q2 * cos).astype(BF16)
            k1 = h[:, D_MODEL + lo:D_MODEL + mid] * k_scale
            k2 = h[:, D_MODEL + mid:D_MODEL + hi] * k_scale
            for dst, kr in ((slice(lo, mid), k1 * cos - k2 * sin),
                            (slice(mid, hi), k1 * sin + k2 * cos)):
                k_ref[tok, dst] = kr.astype(BF16)
                for ci in range(TOKEN_TILE // RET_CHUNK):
                    rows = slice(ci * RET_CHUNK, (ci + 1) * RET_CHUNK)
                    kz_ref[base + ci * RET_CHUNK:base + (ci + 1) * RET_CHUNK, dst] = (
                        kr[rows, :] * zeta_ref[:, dst]).astype(BF16)
        v_ref[tok, :] = h[:, 2 * D_MODEL:2 * D_MODEL + RET_VALUE_WIDTH].astype(BF16)
        gate = h[:, 2 * D_MODEL + RET_VALUE_WIDTH:]
        sg_ref[tok, :] = (gate * jax.nn.sigmoid(gate)).astype(BF16)


def _ret_in(x2d, g, rope, zeta, w_in, layer, make_rope):
    tokens = x2d.shape[0]

    def row(width):
        return pl.BlockSpec((RET_IN_TILE, width), lambda i: (i, 0))

    def out(width, dtype=BF16):
        return jax.ShapeDtypeStruct((tokens, width), dtype)

    rope_b = _resident((1, ROPE_HALF)) if make_rope else row(ROPE_HALF)
    rope_out = 2 if make_rope else 0
    return pl.pallas_call(
        functools.partial(_ret_in_kernel, make_rope=make_rope),
        grid=(tokens // RET_IN_TILE,),
        in_specs=[row(D_MODEL), _resident((1, D_MODEL)), row(ROPE_HALF), rope_b,
                  _resident((RET_CHUNK, D_MODEL)), _layer_resident((D_MODEL, RET_IN_WIDTH), layer)],
        out_specs=[row(D_MODEL), row(D_MODEL), row(D_MODEL), row(RET_VALUE_WIDTH),
                   row(RET_VALUE_WIDTH)] + [row(ROPE_HALF)] * rope_out,
        out_shape=[out(D_MODEL), out(D_MODEL), out(D_MODEL), out(RET_VALUE_WIDTH),
                   out(RET_VALUE_WIDTH)] + [out(ROPE_HALF, F32)] * rope_out,
        compiler_params=_params(1),
        name="ret_in",
    )(x2d, g, *rope, zeta, w_in)


def _ret_core_kernel(x_ref, q_ref, k_ref, kz_ref, v_ref, sg_ref, d_ref, xi_ref, w_out_ref, o_ref,
                     state_ref, y_ref):
    @pl.when(pl.program_id(1) == 0)
    def _():
        state_ref[...] = jnp.zeros_like(state_ref)

    c = RET_CHUNK
    for hd in range(RET_HEADS):
        gamma_chunk = (1.0 - 2.0 ** (-5 - hd)) ** c
        xi = jnp.concatenate([xi_ref[hd]] * (RET_V_DIM // 128), axis=-1)
        qk = slice(hd * RET_QK_DIM, (hd + 1) * RET_QK_DIM)
        vv = slice(hd * RET_V_DIM, (hd + 1) * RET_V_DIM)
        for ci in range(TOKEN_TILE // c):
            rows = slice(ci * c, (ci + 1) * c)
            qc = q_ref[0, rows, qk]
            vc = v_ref[0, rows, vv]
            state = state_ref[hd]
            inner = lax.dot_general(qc, k_ref[0, rows, qk], (((1,), (1,)), ((), ())),
                                    preferred_element_type=F32) * d_ref[hd]
            o = _dot(inner.astype(BF16), vc) + _dot(qc, state.astype(BF16)) * xi
            state_ref[hd] = state * gamma_chunk + lax.dot_general(
                kz_ref[0, rows, qk], vc, (((0,), (0,)), ((), ())), preferred_element_type=F32)
            mu = jnp.mean(o, axis=-1, keepdims=True)
            dev = o - mu
            var = jnp.mean(dev * dev, axis=-1, keepdims=True)
            y = sg_ref[0, rows, vv].astype(F32) * (dev * lax.rsqrt(var + EPS))
            y_ref[rows, vv] = y.astype(BF16)
    o_ref[0] = x_ref[0] + _dot(y_ref[...], w_out_ref[...])


def _ret_core(x3d, q, k, kz, v, sg, d_mat, xi, w_out, layer):
    bsz, seq, _ = x3d.shape

    def row(width):
        return pl.BlockSpec((1, TOKEN_TILE, width), lambda b, i: (b, i, 0))

    return pl.pallas_call(
        _ret_core_kernel,
        grid=(bsz, seq // TOKEN_TILE),
        in_specs=[row(D_MODEL), row(D_MODEL), row(D_MODEL), row(D_MODEL), row(RET_VALUE_WIDTH),
                  row(RET_VALUE_WIDTH), _resident((RET_HEADS, RET_CHUNK, RET_CHUNK)),
                  _resident((RET_HEADS, RET_CHUNK, 128)),
                  _layer_resident((RET_VALUE_WIDTH, D_MODEL), layer)],
        out_specs=row(D_MODEL),
        out_shape=jax.ShapeDtypeStruct(x3d.shape, F32),
        scratch_shapes=[pltpu.VMEM((RET_HEADS, RET_QK_DIM, RET_V_DIM), F32),
                        pltpu.VMEM((TOKEN_TILE, RET_VALUE_WIDTH), BF16)],
        compiler_params=_params(2),
        name="ret_core",
    )(x3d, q, k, kz, v, sg, d_mat, xi, w_out)


def _s5_operator_kernel(rows_ref, cols_ref, b_ref, c_ref, ct_ref, w_ref, z_ref, r_ref, a_ref, m_ref):
    n = S5_CHUNK
    lane = lax.broadcasted_iota(jnp.int32, (1, 2 * S5_STATE), 1)
    sign = jnp.where(lane < S5_STATE, -1.0, 1.0)
    lr, li = rows_ref[0, 0:1, :], rows_ref[0, 1:2, :]
    dt = jnp.exp(rows_ref[0, 2:3, :])
    mag = jnp.exp(lr * dt)
    ar = mag * jnp.cos(li * dt)
    ai = mag * jnp.sin(li * dt)
    den = lr * lr + li * li
    nr = ar - 1.0
    zr = (nr * lr + ai * li) / den
    zi = (ai * lr - nr * li) / den
    bb_ri = zr * b_ref[0, 0] + zi * b_ref[0, 1]
    bb_ir = zr * b_ref[0, 2] + zi * b_ref[0, 3]
    bb_rr = jnp.where(lane < S5_STATE, bb_ri, bb_ir)
    bb_ii = jnp.where(lane < S5_STATE, bb_ir, bb_ri)

    full_mag = jnp.exp(lr * dt * n)
    alr = full_mag * jnp.cos(li * dt * n)
    ali = full_mag * jnp.sin(li * dt * n) * sign
    row = lax.broadcasted_iota(jnp.int32, (8, 2 * S5_STATE), 0)
    a_ref[0] = jnp.where(row == 0, alr, jnp.where(row == 1, ali, jnp.where(row == 2, -ali, 0.0)))

    back = (n - 1 - lax.broadcasted_iota(jnp.int32, (n, 2 * S5_STATE), 0)).astype(F32)
    back_mag = jnp.exp(lr * dt * back)
    pw_r = back_mag * jnp.cos(li * dt * back)
    pw_i = back_mag * jnp.sin(li * dt * back)
    for hi in range(S5_GROUP):
        sl = slice(hi, hi + 1)
        rows = slice(hi * n, (hi + 1) * n)
        w_ref[0, rows, :2 * S5_STATE] = (pw_r * bb_ri[sl] + pw_i * (bb_ir[sl] * sign)).astype(BF16)
        w_ref[0, rows, 2 * S5_STATE:] = (pw_r * bb_ir[sl] - pw_i * (bb_ri[sl] * sign)).astype(BF16)
        m_ref[hi * S5_GROUP:(hi + 1) * S5_GROUP, :] = bb_rr[sl] * c_ref[0, 0] + bb_ii[sl] * c_ref[0, 1]

    fwd = (lax.broadcasted_iota(jnp.int32, (2 * S5_STATE, n), 1) + 1).astype(F32)
    fwd_mag = jnp.exp(cols_ref[0, 0] * dt * fwd)
    p1_r = fwd_mag * jnp.cos(cols_ref[0, 1] * dt * fwd)
    p1_i = fwd_mag * jnp.sin(cols_ref[0, 1] * dt * fwd)
    for ho in range(S5_GROUP):
        z_ref[0, :, ho * n:(ho + 1) * n] = (ct_ref[0, 0][:, ho:ho + 1] * p1_r
                                            + ct_ref[0, 1][:, ho:ho + 1] * p1_i).astype(BF16)
    first = lax.broadcasted_iota(jnp.int32, (2 * S5_STATE, n), 1) == 0
    p0_r = jnp.where(first, 1.0, pltpu.roll(p1_r, 1, 1))
    p0_i = jnp.where(first, 0.0, pltpu.roll(p1_i, 1, 1))
    top = lax.broadcasted_iota(jnp.int32, (2 * S5_STATE, n), 0) < S5_STATE
    r_ref[0] = jnp.dot(m_ref[...], jnp.where(top, p0_r, p0_i), preferred_element_type=F32,
                       precision=lax.Precision.HIGHEST)


def _s5_operators(lam_re, lam_im, log_dt, b_re, b_im, c_re, c_im):
    def pair(u, v):
        return jnp.concatenate([u, v], axis=-1)

    lr, li = lam_re.astype(F32), lam_im.astype(F32)
    lam = jnp.stack([pair(lr, lr), pair(li, li)], axis=1)
    ldt = jnp.broadcast_to(log_dt.astype(F32)[:, None, None], (S5_GROUPS, 1, 2 * S5_STATE))
    rows = jnp.concatenate([lam, ldt, jnp.zeros((S5_GROUPS, 5, 2 * S5_STATE), F32)], axis=1)
    cols = jnp.broadcast_to(lam[..., None], (S5_GROUPS, 2, 2 * S5_STATE, S5_CHUNK))
    br, bi = b_re.astype(F32).transpose(0, 2, 1), b_im.astype(F32).transpose(0, 2, 1)
    b = jnp.stack([pair(br, bi), pair(-bi, br), pair(bi, br), pair(br, -bi)], axis=1)
    cr, ci = c_re.astype(F32), c_im.astype(F32)
    c = jnp.stack([pair(cr, -ci), pair(-ci, -cr)], axis=1)
    ct = c.transpose(0, 1, 3, 2)

    def per_group(*shape):
        return pl.BlockSpec((1,) + shape, lambda g: (g,) + (0,) * len(shape))

    w, z, resp, a = pl.pallas_call(
        _s5_operator_kernel,
        grid=(S5_GROUPS,),
        in_specs=[per_group(8, 2 * S5_STATE), per_group(2, 2 * S5_STATE, S5_CHUNK),
                  per_group(4, S5_GROUP, 2 * S5_STATE), per_group(2, S5_GROUP, 2 * S5_STATE),
                  per_group(2, 2 * S5_STATE, S5_GROUP)],
        out_specs=[per_group(S5_COLS, 4 * S5_STATE), per_group(2 * S5_STATE, S5_COLS),
                   per_group(S5_GROUP * S5_GROUP, S5_CHUNK), per_group(8, 2 * S5_STATE)],
        out_shape=[jax.ShapeDtypeStruct((S5_GROUPS, S5_COLS, 4 * S5_STATE), BF16),
                   jax.ShapeDtypeStruct((S5_GROUPS, 2 * S5_STATE, S5_COLS), BF16),
                   jax.ShapeDtypeStruct((S5_GROUPS, S5_GROUP * S5_GROUP, S5_CHUNK), F32),
                   jax.ShapeDtypeStruct((S5_GROUPS, 8, 2 * S5_STATE), F32)],
        scratch_shapes=[pltpu.VMEM((S5_GROUP * S5_GROUP, 2 * S5_STATE), F32)],
        compiler_params=_params(1),
        name="s5_operators",
    )(rows, cols, b, c, ct)
    return resp, w, z, a


def _channel_rows(ref, ch):
    rows, channels, width = ref.shape
    return ref.reshape(rows * channels, width).at[pl.ds(ch, rows, stride=channels), :]


def _toeplitz_block(r_ref, row, causal):
    n = S5_CHUNK
    lags = jnp.broadcast_to(r_ref[0, pl.ds(row, 1), :], (n, n))
    blk = pltpu.roll(lags, 0, 1, stride=1, stride_axis=0)
    return jnp.where(causal, blk, 0.0).astype(BF16)


def _s5_core_kernel(u_ref, r0_ref, r_next_ref, w_ref, z_ref, a_ref, y_ref, t0_ref, t1_ref, us_ref,
                    v_ref, xin_ref, *, bsz, n_chunks):
    n = S5_CHUNK
    g = pl.program_id(0)
    causal = (lax.broadcasted_iota(jnp.int32, (n, n), 1)
              >= lax.broadcasted_iota(jnp.int32, (n, n), 0))

    @pl.when(g == 0)
    def _():
        def build_rows(hi, carry):
            for ho in range(S5_GROUP):
                t0_ref[pl.ds(pl.multiple_of(hi * n, n), n), ho * n:(ho + 1) * n] = (
                    _toeplitz_block(r0_ref, hi * S5_GROUP + ho, causal))
            return carry

        lax.fori_loop(0, S5_GROUP, build_rows, 0)

    @pl.when(g % 2 == 0)
    def _():
        _s5_group(u_ref, r_next_ref, w_ref, z_ref, a_ref, y_ref, t0_ref, t1_ref, us_ref, v_ref,
                  xin_ref, causal, bsz, n_chunks)

    @pl.when(g % 2 == 1)
    def _():
        _s5_group(u_ref, r_next_ref, w_ref, z_ref, a_ref, y_ref, t1_ref, t0_ref, us_ref, v_ref,
                  xin_ref, causal, bsz, n_chunks)


def _s5_group(u_ref, r_next_ref, w_ref, z_ref, a_ref, y_ref, t_ref, t_next_ref, us_ref, v_ref,
              xin_ref, causal, bsz, n_chunks):
    n = S5_CHUNK
    for hi in range(S5_GROUP):
        for ho in range(S5_GROUP):
            t_next_ref[hi * n:(hi + 1) * n, ho * n:(ho + 1) * n] = (
                _toeplitz_block(r_next_ref, hi * S5_GROUP + ho, causal))

    for hi in range(S5_GROUP):
        us_ref[:, hi * n:(hi + 1) * n] = _channel_rows(u_ref, hi)[...].astype(BF16)
    u = us_ref[...]
    half = 2 * S5_STATE
    v = _dot(u, w_ref[0])
    v_ref[0] = v[:, :half]
    v_ref[1] = v[:, half:]
    a_same = a_ref[0, 0:1, :]
    a_cross_p = a_ref[0, 1:2, :]
    a_cross_q = a_ref[0, 2:3, :]
    p = jnp.zeros((bsz, half), F32)
    q = jnp.zeros((bsz, half), F32)
    for c in range(n_chunks):
        xin_ref[pl.ds(c, bsz, stride=n_chunks), :] = p
        p, q = (a_same * p + a_cross_p * q + v_ref[0, pl.ds(c, bsz, stride=n_chunks), :],
                a_same * q + a_cross_q * p + v_ref[1, pl.ds(c, bsz, stride=n_chunks), :])
    y = _dot(u, t_ref[...]) + _dot(xin_ref[...].astype(BF16), z_ref[0])
    for ho in range(S5_GROUP):
        _channel_rows(y_ref, ho)[...] = y[:, ho * n:(ho + 1) * n]


def _s5_core(u3d, resp, w, z, a, bsz):
    rows = u3d.shape[0]
    chunk_rows = pl.BlockSpec((rows, S5_GROUP, S5_CHUNK), lambda g: (0, g, 0))
    return pl.pallas_call(
        functools.partial(_s5_core_kernel, bsz=bsz, n_chunks=rows // bsz),
        grid=(S5_GROUPS,),
        in_specs=[chunk_rows,
                  pl.BlockSpec((1, S5_GROUP * S5_GROUP, S5_CHUNK), lambda g: (0, 0, 0)),
                  pl.BlockSpec((1, S5_GROUP * S5_GROUP, S5_CHUNK),
                               lambda g: (jnp.minimum(g + 1, S5_GROUPS - 1), 0, 0)),
                  pl.BlockSpec((1, S5_COLS, 4 * S5_STATE), lambda g: (g, 0, 0)),
                  pl.BlockSpec((1, 2 * S5_STATE, S5_COLS), lambda g: (g, 0, 0)),
                  pl.BlockSpec((1, 8, 2 * S5_STATE), lambda g: (g, 0, 0))],
        out_specs=chunk_rows,
        out_shape=jax.ShapeDtypeStruct((rows, D_MODEL, S5_CHUNK), F32),
        scratch_shapes=[pltpu.VMEM((S5_COLS, S5_COLS), BF16),
                        pltpu.VMEM((S5_COLS, S5_COLS), BF16),
                        pltpu.VMEM((rows, S5_COLS), BF16),
                        pltpu.VMEM((2, rows, 2 * S5_STATE), F32),
                        pltpu.VMEM((rows, 2 * S5_STATE), F32)],
        compiler_params=_params(1),
        name="s5_core",
    )(u3d, resp, resp, w, z, a)


def _gelu_tanh(x):
    return 0.5 * x * (1.0 + jnp.tanh(math.sqrt(2.0 / math.pi) * (x + 0.044715 * (x * x * x))))


def _s5_post_kernel(x_ref, y_ref, g_ref, d_ref, w_ref, o_ref, act_ref):
    per_sub = TOKEN_TILE // S5_CHUNK
    for sub in range(S5_TILE_CHUNKS // per_sub):
        tok = slice(sub * TOKEN_TILE, (sub + 1) * TOKEN_TILE)
        x = x_ref[0, tok, :]
        skip = d_ref[...] * _rmsnorm(x, g_ref[...])
        for c in range(per_sub):
            rows = slice(sub * TOKEN_TILE + c * S5_CHUNK, sub * TOKEN_TILE + (c + 1) * S5_CHUNK)
            act_ref[rows, :] = _gelu_tanh(y_ref[sub * per_sub + c].T
                                          + skip[c * S5_CHUNK:(c + 1) * S5_CHUNK, :]).astype(BF16)
        z = _dot(act_ref[tok, :], w_ref[...])
        o_ref[0, tok, :] = x + z[:, :D_MODEL] * jax.nn.sigmoid(z[:, D_MODEL:])


def _s5_post(x3d, y3d, g, d_skip, w_glu, layer):
    bsz, seq, _ = x3d.shape
    tile = S5_TILE_CHUNKS * S5_CHUNK
    steps = seq // tile
    row = pl.BlockSpec((1, tile, D_MODEL), lambda b, i: (b, i, 0))
    return pl.pallas_call(
        _s5_post_kernel,
        grid=(bsz, steps),
        in_specs=[row,
                  pl.BlockSpec((S5_TILE_CHUNKS, D_MODEL, S5_CHUNK), lambda b, i: (b * steps + i, 0, 0)),
                  _resident((1, D_MODEL)), _resident((1, D_MODEL)),
                  _layer_resident((D_MODEL, 2 * D_MODEL), layer)],
        out_specs=row,
        out_shape=jax.ShapeDtypeStruct(x3d.shape, F32),
        scratch_shapes=[pltpu.VMEM((tile, D_MODEL), BF16)],
        compiler_params=_params(2),
        name="s5_post",
    )(x3d, y3d, g, d_skip, w_glu)


def kernel(x, mem, positions, norm_gains, mem_norm, final_norm, ffn1_w_in, ffn1_w_out, ffn2_w_in, ffn2_w_out, s5_lam_re, s5_lam_im, s5_log_dt, s5_b_re, s5_b_im, s5_c_re, s5_c_im, s5_d, s5_w_glu, ret_w_in, ret_w_out, xattn_w_q, xattn_w_kv, xattn_w_o):
    bsz, seq, _ = x.shape
    tokens = bsz * seq
    gains = norm_gains.astype(F32)
    gf = final_norm.astype(F32).reshape(1, D_MODEL)

    kv = _memory_kv(mem.reshape(bsz * N_MEM, D_MODEL), mem_norm.astype(F32).reshape(1, D_MODEL),
                    xattn_w_kv.astype(BF16))
    kv = kv.reshape(DEPTH, bsz, N_MEM, 2 * D_MODEL)
    pos_lanes = jnp.broadcast_to(positions.astype(F32).reshape(tokens, 1), (tokens, ROPE_HALF))
    inv_freq = (1.0 / (ROPE_BASE ** jnp.linspace(0.0, 1.0, ROPE_HALF, dtype=F32))).reshape(1, ROPE_HALF)
    d_mat, xi, zeta = _ret_decay_tables()
    rope_tables = None
    ffn1_w, ffn2_w = (ffn1_w_in.astype(BF16), ffn1_w_out.astype(BF16)), \
                     (ffn2_w_in.astype(BF16), ffn2_w_out.astype(BF16))
    ret_w_in, ret_w_out, s5_w_glu = ret_w_in.astype(BF16), ret_w_out.astype(BF16), s5_w_glu.astype(BF16)
    xattn_w_q, xattn_w_o = xattn_w_q.astype(BF16), xattn_w_o.astype(BF16)

    for i in range(DEPTH):
        g = [gains[i, j].reshape(1, D_MODEL) for j in range(4)]
        ffn1 = functools.partial(_ffn, x.reshape(tokens, D_MODEL), g[0], *ffn1_w, i)
        j = i // 2
        if i % 2 == 0:
            resp, w, z, a = _s5_operators(s5_lam_re[j], s5_lam_im[j], s5_log_dt[j], s5_b_re[j],
                                          s5_b_im[j], s5_c_re[j], s5_c_im[j])
            x2d, u = ffn1(g[1], "s5")
            x = x2d.reshape(bsz, seq, D_MODEL)
            y = _s5_core(u, resp, w, z, a, bsz)
            x = _s5_post(x, y, g[1], s5_d[j].astype(F32).reshape(1, D_MODEL), s5_w_glu, j)
        else:
            x2d, = ffn1(gf)
            if rope_tables is None:
                *heads, cos, sin = _ret_in(x2d, g[1], (pos_lanes, inv_freq), zeta, ret_w_in, j, True)
                rope_tables = (cos, sin)
            else:
                heads = _ret_in(x2d, g[1], rope_tables, zeta, ret_w_in, j, False)
            q, k, kz, v, sg = [t.reshape(bsz, seq, t.shape[-1]) for t in heads]
            x = _ret_core(x2d.reshape(bsz, seq, D_MODEL), q, k, kz, v, sg, d_mat, xi, ret_w_out, j)
        x = _xattn(x, g[2], xattn_w_q, kv, xattn_w_o, i)
        x, = _ffn(x.reshape(tokens, D_MODEL), g[3], *ffn2_w, i, gf,
                  "final" if i == DEPTH - 1 else None)
        x = x.reshape(bsz, seq, D_MODEL)
    return x
```

```python
import functools
import math

import jax
import jax.numpy as jnp
from jax import lax
from jax.experimental import pallas as pl
from jax.experimental.pallas import tpu as pltpu

F32 = jnp.float32
BF16 = jnp.bfloat16

D_MODEL = 1024
DEPTH = 4
N_MEM = 256
D_FF = 2816
MACARON_WEIGHT = 0.5
EPS = 1e-6

S5_GROUP = 16
S5_GROUPS = D_MODEL // S5_GROUP
S5_STATE = 64
S5_CHUNK = 128
S5_COLS = S5_GROUP * S5_CHUNK
S5_TILE_CHUNKS = 8

RET_HEADS = 4
RET_QK_DIM = D_MODEL // RET_HEADS
RET_V_DIM = 2 * RET_QK_DIM
RET_VALUE_WIDTH = RET_HEADS * RET_V_DIM
RET_IN_WIDTH = 2 * D_MODEL + 2 * RET_VALUE_WIDTH
RET_CHUNK = 256
ROPE_BASE = 10000.0
ROPE_HALF = RET_QK_DIM // 2

XATTN_HEADS = 4
XATTN_DIM = D_MODEL // XATTN_HEADS

TOKEN_TILE = 512
FFN_TILE = 1024
FF_BLOCK = 256
GLU_BLOCK = 256
XATTN_TILE = 2048
RET_IN_TILE = 1024
VMEM_LIMIT = 56 * 1024 * 1024


def _params(n_axes):
    return pltpu.CompilerParams(dimension_semantics=("arbitrary",) * n_axes,
                                vmem_limit_bytes=VMEM_LIMIT)


def _resident(shape):
    zeros = (0,) * len(shape)
    return pl.BlockSpec(shape, lambda *_: zeros, pipeline_mode=pl.Buffered(1))


def _layer_resident(shape, layer):
    zeros = (0,) * len(shape)
    return pl.BlockSpec((None,) + shape, lambda *_: (layer,) + zeros, pipeline_mode=pl.Buffered(1))


def _rmsnorm(x, g):
    return x * lax.rsqrt(jnp.mean(x * x, axis=-1, keepdims=True) + EPS) * g


def _dot(a, b):
    return jnp.dot(a, b, preferred_element_type=F32)


def _ffn_kernel(x_ref, g_ref, w_in_ref, w_out_ref, gn_ref, o_ref, *extra_refs, extra):
    *extra_refs, act_ref = extra_refs
    per_sub = TOKEN_TILE // S5_CHUNK
    for sub in range(FFN_TILE // TOKEN_TILE):
        tok = slice(sub * TOKEN_TILE, (sub + 1) * TOKEN_TILE)
        x = x_ref[tok, :]
        xn = _rmsnorm(x, g_ref[...]).astype(BF16)
        for j in range(D_FF // FF_BLOCK):
            cols = slice(j * FF_BLOCK, (j + 1) * FF_BLOCK)
            a = _dot(xn, w_in_ref[:, cols])
            b = _dot(xn, w_in_ref[:, D_FF + j * FF_BLOCK:D_FF + (j + 1) * FF_BLOCK])
            act_ref[tok, cols] = (a * jax.nn.sigmoid(a) * b).astype(BF16)
        y = x + MACARON_WEIGHT * _dot(act_ref[tok, :], w_out_ref[...])
        if extra == "final":
            y = _rmsnorm(y, gn_ref[...])
        o_ref[tok, :] = y
        if extra == "s5":
            u_ref, = extra_refs
            yn = _rmsnorm(y, gn_ref[...])
            for c in range(per_sub):
                u_ref[sub * per_sub + c] = yn[c * S5_CHUNK:(c + 1) * S5_CHUNK, :].T


def _ffn(x2d, g, w_in, w_out, layer, gn, extra=None):
    tokens = x2d.shape[0]
    row = pl.BlockSpec((FFN_TILE, D_MODEL), lambda i: (i, 0))
    out_specs = [row]
    out_shape = [jax.ShapeDtypeStruct((tokens, D_MODEL), F32)]
    if extra == "s5":
        per_tile = FFN_TILE // S5_CHUNK
        out_specs.append(pl.BlockSpec((per_tile, D_MODEL, S5_CHUNK), lambda i: (i, 0, 0)))
        out_shape.append(jax.ShapeDtypeStruct((tokens // S5_CHUNK, D_MODEL, S5_CHUNK), F32))
    return pl.pallas_call(
        functools.partial(_ffn_kernel, extra=extra),
        grid=(tokens // FFN_TILE,),
        in_specs=[row, _resident((1, D_MODEL)), _layer_resident((D_MODEL, 2 * D_FF), layer),
                  _layer_resident((D_FF, D_MODEL), layer), _resident((1, D_MODEL))],
        out_specs=out_specs,
        out_shape=out_shape,
        scratch_shapes=[pltpu.VMEM((FFN_TILE, D_FF), BF16)],
        compiler_params=_params(1),
        name="ffn",
    )(x2d, g, w_in, w_out, gn)


def _kv_kernel(mem_ref, g_ref, w_ref, o_ref):
    mn = _rmsnorm(mem_ref[...], g_ref[...]).astype(BF16)
    o_ref[0] = _dot(mn, w_ref[0]).astype(BF16)


def _memory_kv(mem2d, g, w_kv):
    rows = mem2d.shape[0]
    half = rows // 2
    return pl.pallas_call(
        _kv_kernel,
        grid=(DEPTH, 2),
        in_specs=[pl.BlockSpec((half, D_MODEL), lambda l, r: (r, 0)),
                  _resident((1, D_MODEL)),
                  pl.BlockSpec((1, D_MODEL, 2 * D_MODEL), lambda l, r: (l, 0, 0))],
        out_specs=pl.BlockSpec((1, half, 2 * D_MODEL), lambda l, r: (l, r, 0)),
        out_shape=jax.ShapeDtypeStruct((DEPTH, rows, 2 * D_MODEL), BF16),
        compiler_params=_params(2),
        name="memory_kv",
    )(mem2d, g, w_kv)


def _xattn_kernel(x_ref, g_ref, wq_ref, k_ref, v_ref, wo_ref, o_ref):
    for sub in range(XATTN_TILE // TOKEN_TILE):
        rows = slice(sub * TOKEN_TILE, (sub + 1) * TOKEN_TILE)
        x = x_ref[0, rows, :]
        xn = _rmsnorm(x, g_ref[...]).astype(BF16)
        q = (_dot(xn, wq_ref[...]) * (XATTN_DIM ** -0.5)).astype(BF16)
        heads = []
        for h in range(XATTN_HEADS):
            sl = slice(h * XATTN_DIM, (h + 1) * XATTN_DIM)
            s = lax.dot_general(q[:, sl], k_ref[0, :, sl], (((1,), (1,)), ((), ())),
                                preferred_element_type=F32)
            e = jnp.exp(s - jnp.max(s, axis=-1, keepdims=True))
            p = e / jnp.sum(e, axis=-1, keepdims=True)
            heads.append(_dot(p.astype(BF16), v_ref[0, :, sl]).astype(BF16))
        o = jnp.concatenate(heads, axis=-1)
        o_ref[0, rows, :] = x + _dot(o, wo_ref[...])


def _xattn(x3d, g, w_q, kv, w_o, layer):
    bsz, seq, _ = x3d.shape
    row = pl.BlockSpec((1, XATTN_TILE, D_MODEL), lambda b, i: (b, i, 0))
    return pl.pallas_call(
        _xattn_kernel,
        grid=(bsz, seq // XATTN_TILE),
        in_specs=[row, _resident((1, D_MODEL)), _layer_resident((D_MODEL, D_MODEL), layer),
                  pl.BlockSpec((None, 1, N_MEM, D_MODEL), lambda b, i: (layer, b, 0, 0)),
                  pl.BlockSpec((None, 1, N_MEM, D_MODEL), lambda b, i: (layer, b, 0, 1)),
                  _layer_resident((D_MODEL, D_MODEL), layer)],
        out_specs=row,
        out_shape=jax.ShapeDtypeStruct(x3d.shape, F32),
        compiler_params=_params(2),
        name="xattn",
    )(x3d, g, w_q, kv, kv, w_o)


def _ret_decay_tables():
    c = RET_CHUNK
    log_gamma = jnp.log(1.0 - jnp.exp2(-5.0 - jnp.arange(RET_HEADS, dtype=F32)))
    idx = jnp.arange(c)
    diff = idx[:, None] - idx[None, :]
    d_mat = jnp.where(diff >= 0, jnp.exp(log_gamma[:, None, None]
                                         * jnp.maximum(diff, 0).astype(F32)), 0.0)
    xi = jnp.exp(log_gamma[:, None] * (idx + 1).astype(F32))
    zeta = jnp.exp(log_gamma[:, None] * (c - 1 - idx).astype(F32))
    xi = jnp.broadcast_to(xi[:, :, None], (RET_HEADS, c, 128))
    zeta = jnp.broadcast_to(zeta.T[:, :, None], (c, RET_HEADS, RET_QK_DIM)).reshape(c, D_MODEL)
    return d_mat, xi, zeta


def _ret_in_kernel(x_ref, g_ref, rope_a_ref, rope_b_ref, zeta_ref, w_ref,
                   q_ref, k_ref, kz_ref, v_ref, sg_ref, *rope_out_refs, make_rope):
    if make_rope:
        cos_ref, sin_ref = rope_out_refs
        ang = rope_a_ref[...] * rope_b_ref[...]
        cos_ref[...] = jnp.cos(ang)
        sin_ref[...] = jnp.sin(ang)
    else:
        cos_ref, sin_ref = rope_a_ref, rope_b_ref
    k_scale = RET_QK_DIM ** -0.5
    for sub in range(RET_IN_TILE // TOKEN_TILE):
        base = sub * TOKEN_TILE
        tok = slice(base, base + TOKEN_TILE)
        xn = _rmsnorm(x_ref[tok, :], g_ref[...]).astype(BF16)
        cos = cos_ref[tok, :]
        sin = sin_ref[tok, :]
        for hd in range(RET_HEADS):
            lo = hd * RET_QK_DIM
            mid = lo + ROPE_HALF
            hi = lo + RET_QK_DIM
            qh = _dot(xn, w_ref[:, lo:hi])
            q1, q2 = qh[:, :ROPE_HALF], qh[:, ROPE_HALF:]
            q_ref[tok, lo:mid] = (q1 * cos - q2 * sin).astype(BF16)
            q_ref[tok, mid:hi] = (q1 * sin + q2 * cos).astype(BF16)
            kh = _dot(xn, w_ref[:, D_MODEL + lo:D_MODEL + hi]) * k_scale
            k1, k2 = kh[:, :ROPE_HALF], kh[:, ROPE_HALF:]
            for dst, kr in ((slice(lo, mid), k1 * cos - k2 * sin),
                            (slice(mid, hi), k1 * sin + k2 * cos)):
                k_ref[tok, dst] = kr.astype(BF16)
                for ci in range(TOKEN_TILE // RET_CHUNK):
                    rows = slice(ci * RET_CHUNK, (ci + 1) * RET_CHUNK)
                    kz_ref[base + ci * RET_CHUNK:base + (ci + 1) * RET_CHUNK, dst] = (
                        kr[rows, :] * zeta_ref[:, dst]).astype(BF16)
        for blk in range(RET_VALUE_WIDTH // RET_QK_DIM):
            cols = slice(blk * RET_QK_DIM, (blk + 1) * RET_QK_DIM)
            v_ref[tok, cols] = _dot(xn, w_ref[:, 2 * D_MODEL + blk * RET_QK_DIM:
                                              2 * D_MODEL + (blk + 1) * RET_QK_DIM]).astype(BF16)
            gate = _dot(xn, w_ref[:, 2 * D_MODEL + RET_VALUE_WIDTH + blk * RET_QK_DIM:
                                  2 * D_MODEL + RET_VALUE_WIDTH + (blk + 1) * RET_QK_DIM])
            sg_ref[tok, cols] = (gate * jax.nn.sigmoid(gate)).astype(BF16)


def _ret_in(x2d, g, rope, zeta, w_in, layer, make_rope):
    tokens = x2d.shape[0]

    def row(width):
        return pl.BlockSpec((RET_IN_TILE, width), lambda i: (i, 0))

    def out(width, dtype=BF16):
        return jax.ShapeDtypeStruct((tokens, width), dtype)

    rope_b = _resident((1, ROPE_HALF)) if make_rope else row(ROPE_HALF)
    rope_out = 2 if make_rope else 0
    return pl.pallas_call(
        functools.partial(_ret_in_kernel, make_rope=make_rope),
        grid=(tokens // RET_IN_TILE,),
        in_specs=[row(D_MODEL), _resident((1, D_MODEL)), row(ROPE_HALF), rope_b,
                  _resident((RET_CHUNK, D_MODEL)), _layer_resident((D_MODEL, RET_IN_WIDTH), layer)],
        out_specs=[row(D_MODEL), row(D_MODEL), row(D_MODEL), row(RET_VALUE_WIDTH),
                   row(RET_VALUE_WIDTH)] + [row(ROPE_HALF)] * rope_out,
        out_shape=[out(D_MODEL), out(D_MODEL), out(D_MODEL), out(RET_VALUE_WIDTH),
                   out(RET_VALUE_WIDTH)] + [out(ROPE_HALF, F32)] * rope_out,
        compiler_params=_params(1),
        name="ret_in",
    )(x2d, g, *rope, zeta, w_in)


def _ret_core_kernel(x_ref, q_ref, k_ref, kz_ref, v_ref, sg_ref, d_ref, xi_ref, w_out_ref, o_ref,
                     state_ref, y_ref):
    @pl.when(pl.program_id(1) == 0)
    def _():
        state_ref[...] = jnp.zeros_like(state_ref)

    c = RET_CHUNK
    for hd in range(RET_HEADS):
        gamma_chunk = (1.0 - 2.0 ** (-5 - hd)) ** c
        xi = jnp.concatenate([xi_ref[hd]] * (RET_V_DIM // 128), axis=-1)
        qk = slice(hd * RET_QK_DIM, (hd + 1) * RET_QK_DIM)
        vv = slice(hd * RET_V_DIM, (hd + 1) * RET_V_DIM)
        for ci in range(TOKEN_TILE // c):
            rows = slice(ci * c, (ci + 1) * c)
            qc = q_ref[0, rows, qk]
            vc = v_ref[0, rows, vv]
            state = state_ref[hd]
            inner = lax.dot_general(qc, k_ref[0, rows, qk], (((1,), (1,)), ((), ())),
                                    preferred_element_type=F32) * d_ref[hd]
            o = _dot(inner.astype(BF16), vc) + _dot(qc, state.astype(BF16)) * xi
            state_ref[hd] = state * gamma_chunk + lax.dot_general(
                kz_ref[0, rows, qk], vc, (((0,), (0,)), ((), ())), preferred_element_type=F32)
            mu = jnp.mean(o, axis=-1, keepdims=True)
            dev = o - mu
            var = jnp.mean(dev * dev, axis=-1, keepdims=True)
            y = sg_ref[0, rows, vv].astype(F32) * (dev * lax.rsqrt(var + EPS))
            y_ref[rows, vv] = y.astype(BF16)
    o_ref[0] = x_ref[0] + _dot(y_ref[...], w_out_ref[...])


def _ret_core(x3d, q, k, kz, v, sg, d_mat, xi, w_out, layer):
    bsz, seq, _ = x3d.shape

    def row(width):
        return pl.BlockSpec((1, TOKEN_TILE, width), lambda b, i: (b, i, 0))

    return pl.pallas_call(
        _ret_core_kernel,
        grid=(bsz, seq // TOKEN_TILE),
        in_specs=[row(D_MODEL), row(D_MODEL), row(D_MODEL), row(D_MODEL), row(RET_VALUE_WIDTH),
                  row(RET_VALUE_WIDTH), _resident((RET_HEADS, RET_CHUNK, RET_CHUNK)),
                  _resident((RET_HEADS, RET_CHUNK, 128)),
                  _layer_resident((RET_VALUE_WIDTH, D_MODEL), layer)],
        out_specs=row(D_MODEL),
        out_shape=jax.ShapeDtypeStruct(x3d.shape, F32),
        scratch_shapes=[pltpu.VMEM((RET_HEADS, RET_QK_DIM, RET_V_DIM), F32),
                        pltpu.VMEM((TOKEN_TILE, RET_VALUE_WIDTH), BF16)],
        compiler_params=_params(2),
        name="ret_core",
    )(x3d, q, k, kz, v, sg, d_mat, xi, w_out)


def _s5_operator_kernel(rows_ref, cols_ref, b_ref, c_ref, ct_ref, w_ref, z_ref, r_ref, a_ref, m_ref):
    n = S5_CHUNK
    lane = lax.broadcasted_iota(jnp.int32, (1, 2 * S5_STATE), 1)
    sign = jnp.where(lane < S5_STATE, -1.0, 1.0)
    lr, li = rows_ref[0, 0:1, :], rows_ref[0, 1:2, :]
    dt = jnp.exp(rows_ref[0, 2:3, :])
    mag = jnp.exp(lr * dt)
    ar = mag * jnp.cos(li * dt)
    ai = mag * jnp.sin(li * dt)
    den = lr * lr + li * li
    nr = ar - 1.0
    zr = (nr * lr + ai * li) / den
    zi = (ai * lr - nr * li) / den
    bb_ri = zr * b_ref[0, 0] + zi * b_ref[0, 1]
    bb_ir = zr * b_ref[0, 2] + zi * b_ref[0, 3]
    bb_rr = jnp.where(lane < S5_STATE, bb_ri, bb_ir)
    bb_ii = jnp.where(lane < S5_STATE, bb_ir, bb_ri)

    full_mag = jnp.exp(lr * dt * n)
    alr = full_mag * jnp.cos(li * dt * n)
    ali = full_mag * jnp.sin(li * dt * n) * sign
    row = lax.broadcasted_iota(jnp.int32, (8, 2 * S5_STATE), 0)
    a_ref[0] = jnp.where(row == 0, alr, jnp.where(row == 1, ali, jnp.where(row == 2, -ali, 0.0)))

    back = (n - 1 - lax.broadcasted_iota(jnp.int32, (n, 2 * S5_STATE), 0)).astype(F32)
    back_mag = jnp.exp(lr * dt * back)
    pw_r = back_mag * jnp.cos(li * dt * back)
    pw_i = back_mag * jnp.sin(li * dt * back)
    for hi in range(S5_GROUP):
        sl = slice(hi, hi + 1)
        rows = slice(hi * n, (hi + 1) * n)
        w_ref[0, rows, :2 * S5_STATE] = (pw_r * bb_ri[sl] + pw_i * (bb_ir[sl] * sign)).astype(BF16)
        w_ref[0, rows, 2 * S5_STATE:] = (pw_r * bb_ir[sl] - pw_i * (bb_ri[sl] * sign)).astype(BF16)
        m_ref[hi * S5_GROUP:(hi + 1) * S5_GROUP, :] = bb_rr[sl] * c_ref[0, 0] + bb_ii[sl] * c_ref[0, 1]

    fwd = (lax.broadcasted_iota(jnp.int32, (2 * S5_STATE, n), 1) + 1).astype(F32)
    fwd_mag = jnp.exp(cols_ref[0, 0] * dt * fwd)
    p1_r = fwd_mag * jnp.cos(cols_ref[0, 1] * dt * fwd)
    p1_i = fwd_mag * jnp.sin(cols_ref[0, 1] * dt * fwd)
    for ho in range(S5_GROUP):
        z_ref[0, :, ho * n:(ho + 1) * n] = (ct_ref[0, 0][:, ho:ho + 1] * p1_r
                                            + ct_ref[0, 1][:, ho:ho + 1] * p1_i).astype(BF16)
    first = lax.broadcasted_iota(jnp.int32, (2 * S5_STATE, n), 1) == 0
    p0_r = jnp.where(first, 1.0, pltpu.roll(p1_r, 1, 1))
    p0_i = jnp.where(first, 0.0, pltpu.roll(p1_i, 1, 1))
    top = lax.broadcasted_iota(jnp.int32, (2 * S5_STATE, n), 0) < S5_STATE
    r_ref[0] = jnp.dot(m_ref[...], jnp.where(top, p0_r, p0_i), preferred_element_type=F32,
                       precision=lax.Precision.HIGHEST)


def _s5_operators(lam_re, lam_im, log_dt, b_re, b_im, c_re, c_im):
    def pair(u, v):
        return jnp.concatenate([u, v], axis=-1)

    lr, li = lam_re.astype(F32), lam_im.astype(F32)
    lam = jnp.stack([pair(lr, lr), pair(li, li)], axis=1)
    ldt = jnp.broadcast_to(log_dt.astype(F32)[:, None, None], (S5_GROUPS, 1, 2 * S5_STATE))
    rows = jnp.concatenate([lam, ldt, jnp.zeros((S5_GROUPS, 5, 2 * S5_STATE), F32)], axis=1)
    cols = jnp.broadcast_to(lam[..., None], (S5_GROUPS, 2, 2 * S5_STATE, S5_CHUNK))
    br, bi = b_re.astype(F32).transpose(0, 2, 1), b_im.astype(F32).transpose(0, 2, 1)
    b = jnp.stack([pair(br, bi), pair(-bi, br), pair(bi, br), pair(br, -bi)], axis=1)
    cr, ci = c_re.astype(F32), c_im.astype(F32)
    c = jnp.stack([pair(cr, -ci), pair(-ci, -cr)], axis=1)
    ct = c.transpose(0, 1, 3, 2)

    def per_group(*shape):
        return pl.BlockSpec((1,) + shape, lambda g: (g,) + (0,) * len(shape))

    w, z, resp, a = pl.pallas_call(
        _s5_operator_kernel,
        grid=(S5_GROUPS,),
        in_specs=[per_group(8, 2 * S5_STATE), per_group(2, 2 * S5_STATE, S5_CHUNK),
                  per_group(4, S5_GROUP, 2 * S5_STATE), per_group(2, S5_GROUP, 2 * S5_STATE),
                  per_group(2, 2 * S5_STATE, S5_GROUP)],
        out_specs=[per_group(S5_COLS, 4 * S5_STATE), per_group(2 * S5_STATE, S5_COLS),
                   per_group(S5_GROUP * S5_GROUP, S5_CHUNK), per_group(8, 2 * S5_STATE)],
        out_shape=[jax.ShapeDtypeStruct((S5_GROUPS, S5_COLS, 4 * S5_STATE), BF16),
                   jax.ShapeDtypeStruct((S5_GROUPS, 2 * S5_STATE, S5_COLS), BF16),
                   jax.ShapeDtypeStruct((S5_GROUPS, S5_GROUP * S5_GROUP, S5_CHUNK), F32),
                   jax.ShapeDtypeStruct((S5_GROUPS, 8, 2 * S5_STATE), F32)],
        scratch_shapes=[pltpu.VMEM((S5_GROUP * S5_GROUP, 2 * S5_STATE), F32)],
        compiler_params=_params(1),
        name="s5_operators",
    )(rows, cols, b, c, ct)
    return resp, w, z, a


def _channel_rows(ref, ch):
    rows, channels, width = ref.shape
    return ref.reshape(rows * channels, width).at[pl.ds(ch, rows, stride=channels), :]


def _toeplitz_block(r_ref, row, causal):
    n = S5_CHUNK
    lags = jnp.broadcast_to(r_ref[0, pl.ds(row, 1), :], (n, n))
    blk = pltpu.roll(lags, 0, 1, stride=1, stride_axis=0)
    return jnp.where(causal, blk, 0.0).astype(BF16)


def _s5_core_kernel(u_ref, r0_ref, r_next_ref, w_ref, z_ref, a_ref, y_ref, t0_ref, t1_ref, us_ref,
                    v_ref, xin_ref, *, bsz, n_chunks):
    n = S5_CHUNK
    g = pl.program_id(0)
    causal = (lax.broadcasted_iota(jnp.int32, (n, n), 1)
              >= lax.broadcasted_iota(jnp.int32, (n, n), 0))

    @pl.when(g == 0)
    def _():
        def build_rows(hi, carry):
            for ho in range(S5_GROUP):
                t0_ref[pl.ds(pl.multiple_of(hi * n, n), n), ho * n:(ho + 1) * n] = (
                    _toeplitz_block(r0_ref, hi * S5_GROUP + ho, causal))
            return carry

        lax.fori_loop(0, S5_GROUP, build_rows, 0)

    @pl.when(g % 2 == 0)
    def _():
        _s5_group(u_ref, r_next_ref, w_ref, z_ref, a_ref, y_ref, t0_ref, t1_ref, us_ref, v_ref,
                  xin_ref, causal, bsz, n_chunks)

    @pl.when(g % 2 == 1)
    def _():
        _s5_group(u_ref, r_next_ref, w_ref, z_ref, a_ref, y_ref, t1_ref, t0_ref, us_ref, v_ref,
                  xin_ref, causal, bsz, n_chunks)


def _s5_group(u_ref, r_next_ref, w_ref, z_ref, a_ref, y_ref, t_ref, t_next_ref, us_ref, v_ref,
              xin_ref, causal, bsz, n_chunks):
    n = S5_CHUNK
    for hi in range(S5_GROUP):
        for ho in range(S5_GROUP):
            t_next_ref[hi * n:(hi + 1) * n, ho * n:(ho + 1) * n] = (
                _toeplitz_block(r_next_ref, hi * S5_GROUP + ho, causal))

    for hi in range(S5_GROUP):
        us_ref[:, hi * n:(hi + 1) * n] = _channel_rows(u_ref, hi)[...].astype(BF16)
    u = us_ref[...]
    half = 2 * S5_STATE
    v = _dot(u, w_ref[0])
    v_ref[0] = v[:, :half]
    v_ref[1] = v[:, half:]
    a_same = a_ref[0, 0:1, :]
    a_cross_p = a_ref[0, 1:2, :]
    a_cross_q = a_ref[0, 2:3, :]
    p = jnp.zeros((bsz, half), F32)
    q = jnp.zeros((bsz, half), F32)
    for c in range(n_chunks):
        xin_ref[pl.ds(c, bsz, stride=n_chunks), :] = p
        p, q = (a_same * p + a_cross_p * q + v_ref[0, pl.ds(c, bsz, stride=n_chunks), :],
                a_same * q + a_cross_q * p + v_ref[1, pl.ds(c, bsz, stride=n_chunks), :])
    y = _dot(u, t_ref[...]) + _dot(xin_ref[...].astype(BF16), z_ref[0])
    for ho in range(S5_GROUP):
        _channel_rows(y_ref, ho)[...] = y[:, ho * n:(ho + 1) * n]


def _s5_core(u3d, resp, w, z, a, bsz):
    rows = u3d.shape[0]
    chunk_rows = pl.BlockSpec((rows, S5_GROUP, S5_CHUNK), lambda g: (0, g, 0))
    return pl.pallas_call(
        functools.partial(_s5_core_kernel, bsz=bsz, n_chunks=rows // bsz),
        grid=(S5_GROUPS,),
        in_specs=[chunk_rows,
                  pl.BlockSpec((1, S5_GROUP * S5_GROUP, S5_CHUNK), lambda g: (0, 0, 0)),
                  pl.BlockSpec((1, S5_GROUP * S5_GROUP, S5_CHUNK),
                               lambda g: (jnp.minimum(g + 1, S5_GROUPS - 1), 0, 0)),
                  pl.BlockSpec((1, S5_COLS, 4 * S5_STATE), lambda g: (g, 0, 0)),
                  pl.BlockSpec((1, 2 * S5_STATE, S5_COLS), lambda g: (g, 0, 0)),
                  pl.BlockSpec((1, 8, 2 * S5_STATE), lambda g: (g, 0, 0))],
        out_specs=chunk_rows,
        out_shape=jax.ShapeDtypeStruct((rows, D_MODEL, S5_CHUNK), F32),
        scratch_shapes=[pltpu.VMEM((S5_COLS, S5_COLS), BF16),
                        pltpu.VMEM((S5_COLS, S5_COLS), BF16),
                        pltpu.VMEM((rows, S5_COLS), BF16),
                        pltpu.VMEM((2, rows, 2 * S5_STATE), F32),
                        pltpu.VMEM((rows, 2 * S5_STATE), F32)],
        compiler_params=_params(1),
        name="s5_core",
    )(u3d, resp, resp, w, z, a)


def _gelu_tanh(x):
    return 0.5 * x * (1.0 + jnp.tanh(math.sqrt(2.0 / math.pi) * (x + 0.044715 * (x * x * x))))


def _s5_post_kernel(x_ref, y_ref, g_ref, d_ref, w_ref, o_ref, act_ref):
    per_sub = TOKEN_TILE // S5_CHUNK
    for sub in range(S5_TILE_CHUNKS // per_sub):
        tok = slice(sub * TOKEN_TILE, (sub + 1) * TOKEN_TILE)
        x = x_ref[0, tok, :]
        skip = d_ref[...] * _rmsnorm(x, g_ref[...])
        for c in range(per_sub):
            rows = slice(sub * TOKEN_TILE + c * S5_CHUNK, sub * TOKEN_TILE + (c + 1) * S5_CHUNK)
            act_ref[rows, :] = _gelu_tanh(y_ref[sub * per_sub + c].T
                                          + skip[c * S5_CHUNK:(c + 1) * S5_CHUNK, :]).astype(BF16)
        act = act_ref[tok, :]
        for blk in range(D_MODEL // GLU_BLOCK):
            cols = slice(blk * GLU_BLOCK, (blk + 1) * GLU_BLOCK)
            a = _dot(act, w_ref[:, cols])
            gate = _dot(act, w_ref[:, D_MODEL + blk * GLU_BLOCK:D_MODEL + (blk + 1) * GLU_BLOCK])
            o_ref[0, tok, cols] = x[:, cols] + a * jax.nn.sigmoid(gate)


def _s5_post(x3d, y3d, g, d_skip, w_glu, layer):
    bsz, seq, _ = x3d.shape
    tile = S5_TILE_CHUNKS * S5_CHUNK
    steps = seq // tile
    row = pl.BlockSpec((1, tile, D_MODEL), lambda b, i: (b, i, 0))
    return pl.pallas_call(
        _s5_post_kernel,
        grid=(bsz, steps),
        in_specs=[row,
                  pl.BlockSpec((S5_TILE_CHUNKS, D_MODEL, S5_CHUNK), lambda b, i: (b * steps + i, 0, 0)),
                  _resident((1, D_MODEL)), _resident((1, D_MODEL)),
                  _layer_resident((D_MODEL, 2 * D_MODEL), layer)],
        out_specs=row,
        out_shape=jax.ShapeDtypeStruct(x3d.shape, F32),
        scratch_shapes=[pltpu.VMEM((tile, D_MODEL), BF16)],
        compiler_params=_params(2),
        name="s5_post",
    )(x3d, y3d, g, d_skip, w_glu)


def kernel(x, mem, positions, norm_gains, mem_norm, final_norm, ffn1_w_in, ffn1_w_out, ffn2_w_in, ffn2_w_out, s5_lam_re, s5_lam_im, s5_log_dt, s5_b_re, s5_b_im, s5_c_re, s5_c_im, s5_d, s5_w_glu, ret_w_in, ret_w_out, xattn_w_q, xattn_w_kv, xattn_w_o):
    bsz, seq, _ = x.shape
    tokens = bsz * seq
    gains = norm_gains.astype(F32)
    gf = final_norm.astype(F32).reshape(1, D_MODEL)

    kv = _memory_kv(mem.reshape(bsz * N_MEM, D_MODEL), mem_norm.astype(F32).reshape(1, D_MODEL),
                    xattn_w_kv.astype(BF16))
    kv = kv.reshape(DEPTH, bsz, N_MEM, 2 * D_MODEL)
    pos_lanes = jnp.broadcast_to(positions.astype(F32).reshape(tokens, 1), (tokens, ROPE_HALF))
    inv_freq = (1.0 / (ROPE_BASE ** jnp.linspace(0.0, 1.0, ROPE_HALF, dtype=F32))).reshape(1, ROPE_HALF)
    d_mat, xi, zeta = _ret_decay_tables()
    rope_tables = None
    ffn1_w, ffn2_w = (ffn1_w_in.astype(BF16), ffn1_w_out.astype(BF16)), \
                     (ffn2_w_in.astype(BF16), ffn2_w_out.astype(BF16))
    ret_w_in, ret_w_out, s5_w_glu = ret_w_in.astype(BF16), ret_w_out.astype(BF16), s5_w_glu.astype(BF16)
    xattn_w_q, xattn_w_o = xattn_w_q.astype(BF16), xattn_w_o.astype(BF16)

    for i in range(DEPTH):
        g = [gains[i, j].reshape(1, D_MODEL) for j in range(4)]
        ffn1 = functools.partial(_ffn, x.reshape(tokens, D_MODEL), g[0], *ffn1_w, i)
        j = i // 2
        if i % 2 == 0:
            resp, w, z, a = _s5_operators(s5_lam_re[j], s5_lam_im[j], s5_log_dt[j], s5_b_re[j],
                                          s5_b_im[j], s5_c_re[j], s5_c_im[j])
            x2d, u = ffn1(g[1], "s5")
            x = x2d.reshape(bsz, seq, D_MODEL)
            y = _s5_core(u, resp, w, z, a, bsz)
            x = _s5_post(x, y, g[1], s5_d[j].astype(F32).reshape(1, D_MODEL), s5_w_glu, j)
        else:
            x2d, = ffn1(gf)
            if rope_tables is None:
                *heads, cos, sin = _ret_in(x2d, g[1], (pos_lanes, inv_freq), zeta, ret_w_in, j, True)
                rope_tables = (cos, sin)
            else:
                heads = _ret_in(x2d, g[1], rope_tables, zeta, ret_w_in, j, False)
            q, k, kz, v, sg = [t.reshape(bsz, seq, t.shape[-1]) for t in heads]
            x = _ret_core(x2d.reshape(bsz, seq, D_MODEL), q, k, kz, v, sg, d_mat, xi, ret_w_out, j)
        x = _xattn(x, g[2], xattn_w_q, kv, xattn_w_o, i)
        x, = _ffn(x.reshape(tokens, D_MODEL), g[3], *ffn2_w, i, gf,
                  "final" if i == DEPTH - 1 else None)
        x = x.reshape(bsz, seq, D_MODEL)
    return x
```

```python
import functools
import math

import jax
import jax.numpy as jnp
from jax import lax
from jax.experimental import pallas as pl
from jax.experimental.pallas import tpu as pltpu

F32 = jnp.float32
BF16 = jnp.bfloat16

D_MODEL = 1024
DEPTH = 4
N_MEM = 256
D_FF = 2816
MACARON_WEIGHT = 0.5
EPS = 1e-6

S5_GROUP = 16
S5_GROUPS = D_MODEL // S5_GROUP
S5_STATE = 64
S5_CHUNK = 128
S5_COLS = S5_GROUP * S5_CHUNK
S5_TILE_CHUNKS = 8

RET_HEADS = 4
RET_QK_DIM = D_MODEL // RET_HEADS
RET_V_DIM = 2 * RET_QK_DIM
RET_VALUE_WIDTH = RET_HEADS * RET_V_DIM
RET_IN_WIDTH = 2 * D_MODEL + 2 * RET_VALUE_WIDTH
RET_CHUNK = 256
ROPE_BASE = 10000.0
ROPE_HALF = RET_QK_DIM // 2

XATTN_HEADS = 4
XATTN_DIM = D_MODEL // XATTN_HEADS

TOKEN_TILE = 512
FFN_TILE = 1024
FF_BLOCK = 256
GLU_BLOCK = 256
XATTN_TILE = 2048
RET_IN_TILE = 1024
VMEM_LIMIT = 56 * 1024 * 1024


def _params(n_axes):
    return pltpu.CompilerParams(dimension_semantics=("arbitrary",) * n_axes,
                                vmem_limit_bytes=VMEM_LIMIT)


def _resident(shape):
    zeros = (0,) * len(shape)
    return pl.BlockSpec(shape, lambda *_: zeros, pipeline_mode=pl.Buffered(1))


def _layer_resident(shape, layer):
    zeros = (0,) * len(shape)
    return pl.BlockSpec((None,) + shape, lambda *_: (layer,) + zeros, pipeline_mode=pl.Buffered(1))


def _rmsnorm(x, g):
    return x * lax.rsqrt(jnp.mean(x * x, axis=-1, keepdims=True) + EPS) * g


def _dot(a, b):
    return jnp.dot(a, b, preferred_element_type=F32)


def _ffn_kernel(x_ref, g_ref, w_in_ref, w_out_ref, gn_ref, o_ref, *extra_refs, extra):
    *extra_refs, act_ref = extra_refs
    per_sub = TOKEN_TILE // S5_CHUNK
    for sub in range(FFN_TILE // TOKEN_TILE):
        tok = slice(sub * TOKEN_TILE, (sub + 1) * TOKEN_TILE)
        x = x_ref[tok, :]
        xn = _rmsnorm(x, g_ref[...]).astype(BF16)
        for j in range(D_FF // FF_BLOCK):
            cols = slice(j * FF_BLOCK, (j + 1) * FF_BLOCK)
            a = _dot(xn, w_in_ref[:, cols])
            b = _dot(xn, w_in_ref[:, D_FF + j * FF_BLOCK:D_FF + (j + 1) * FF_BLOCK])
            act_ref[tok, cols] = (a * jax.nn.sigmoid(a) * b).astype(BF16)
        y = x + MACARON_WEIGHT * _dot(act_ref[tok, :], w_out_ref[...])
        if extra == "final":
            y = _rmsnorm(y, gn_ref[...])
        o_ref[tok, :] = y
        if extra == "s5":
            u_ref, = extra_refs
            yn = _rmsnorm(y, gn_ref[...])
            for c in range(per_sub):
                u_ref[sub * per_sub + c] = yn[c * S5_CHUNK:(c + 1) * S5_CHUNK, :].T


def _ffn(x2d, g, w_in, w_out, layer, gn, extra=None):
    tokens = x2d.shape[0]
    row = pl.BlockSpec((FFN_TILE, D_MODEL), lambda i: (i, 0))
    out_specs = [row]
    out_shape = [jax.ShapeDtypeStruct((tokens, D_MODEL), F32)]
    if extra == "s5":
        per_tile = FFN_TILE // S5_CHUNK
        out_specs.append(pl.BlockSpec((per_tile, D_MODEL, S5_CHUNK), lambda i: (i, 0, 0)))
        out_shape.append(jax.ShapeDtypeStruct((tokens // S5_CHUNK, D_MODEL, S5_CHUNK), F32))
    return pl.pallas_call(
        functools.partial(_ffn_kernel, extra=extra),
        grid=(tokens // FFN_TILE,),
        in_specs=[row, _resident((1, D_MODEL)), _layer_resident((D_MODEL, 2 * D_FF), layer),
                  _layer_resident((D_FF, D_MODEL), layer), _resident((1, D_MODEL))],
        out_specs=out_specs,
        out_shape=out_shape,
        scratch_shapes=[pltpu.VMEM((FFN_TILE, D_FF), BF16)],
        compiler_params=_params(1),
        name="ffn",
    )(x2d, g, w_in, w_out, gn)


def _kv_kernel(mem_ref, g_ref, w_ref, o_ref):
    mn = _rmsnorm(mem_ref[...], g_ref[...]).astype(BF16)
    o_ref[0] = _dot(mn, w_ref[0]).astype(BF16)


def _memory_kv(mem2d, g, w_kv):
    rows = mem2d.shape[0]
    half = rows // 2
    return pl.pallas_call(
        _kv_kernel,
        grid=(DEPTH, 2),
        in_specs=[pl.BlockSpec((half, D_MODEL), lambda l, r: (r, 0)),
                  _resident((1, D_MODEL)),
                  pl.BlockSpec((1, D_MODEL, 2 * D_MODEL), lambda l, r: (l, 0, 0))],
        out_specs=pl.BlockSpec((1, half, 2 * D_MODEL), lambda l, r: (l, r, 0)),
        out_shape=jax.ShapeDtypeStruct((DEPTH, rows, 2 * D_MODEL), BF16),
        compiler_params=_params(2),
        name="memory_kv",
    )(mem2d, g, w_kv)


def _xattn_kernel(x_ref, g_ref, wq_ref, k_ref, v_ref, wo_ref, o_ref):
    for sub in range(XATTN_TILE // TOKEN_TILE):
        rows = slice(sub * TOKEN_TILE, (sub + 1) * TOKEN_TILE)
        x = x_ref[0, rows, :]
        xn = _rmsnorm(x, g_ref[...]).astype(BF16)
        q = (_dot(xn, wq_ref[...]) * (XATTN_DIM ** -0.5)).astype(BF16)
        heads = []
        for h in range(XATTN_HEADS):
            sl = slice(h * XATTN_DIM, (h + 1) * XATTN_DIM)
            s = lax.dot_general(q[:, sl], k_ref[0, :, sl], (((1,), (1,)), ((), ())),
                                preferred_element_type=F32)
            e = jnp.exp(s - jnp.max(s, axis=-1, keepdims=True))
            p = e / jnp.sum(e, axis=-1, keepdims=True)
            heads.append(_dot(p.astype(BF16), v_ref[0, :, sl]).astype(BF16))
        o = jnp.concatenate(heads, axis=-1)
        o_ref[0, rows, :] = x + _dot(o, wo_ref[...])


def _xattn(x3d, g, w_q, kv, w_o, layer):
    bsz, seq, _ = x3d.shape
    row = pl.BlockSpec((1, XATTN_TILE, D_MODEL), lambda b, i: (b, i, 0))
    return pl.pallas_call(
        _xattn_kernel,
        grid=(bsz, seq // XATTN_TILE),
        in_specs=[row, _resident((1, D_MODEL)), _layer_resident((D_MODEL, D_MODEL), layer),
                  pl.BlockSpec((None, 1, N_MEM, D_MODEL), lambda b, i: (layer, b, 0, 0)),
                  pl.BlockSpec((None, 1, N_MEM, D_MODEL), lambda b, i: (layer, b, 0, 1)),
                  _layer_resident((D_MODEL, D_MODEL), layer)],
        out_specs=row,
        out_shape=jax.ShapeDtypeStruct(x3d.shape, F32),
        compiler_params=_params(2),
        name="xattn",
    )(x3d, g, w_q, kv, kv, w_o)


def _ret_decay_tables():
    c = RET_CHUNK
    log_gamma = jnp.log(1.0 - jnp.exp2(-5.0 - jnp.arange(RET_HEADS, dtype=F32)))
    idx = jnp.arange(c)
    diff = idx[:, None] - idx[None, :]
    d_mat = jnp.where(diff >= 0, jnp.exp(log_gamma[:, None, None]
                                         * jnp.maximum(diff, 0).astype(F32)), 0.0)
    xi = jnp.exp(log_gamma[:, None] * (idx + 1).astype(F32))
    zeta = jnp.exp(log_gamma[:, None] * (c - 1 - idx).astype(F32))
    xi = jnp.broadcast_to(xi[:, :, None], (RET_HEADS, c, 128))
    zeta = jnp.broadcast_to(zeta.T[:, :, None], (c, RET_HEADS, RET_QK_DIM)).reshape(c, D_MODEL)
    return d_mat, xi, zeta


def _ret_in_kernel(x_ref, g_ref, rope_a_ref, rope_b_ref, zeta_ref, w_ref,
                   q_ref, k_ref, kz_ref, v_ref, sg_ref, *rope_out_refs, make_rope):
    if make_rope:
        cos_ref, sin_ref = rope_out_refs
        ang = rope_a_ref[...] * rope_b_ref[...]
        cos_ref[...] = jnp.cos(ang)
        sin_ref[...] = jnp.sin(ang)
    else:
        cos_ref, sin_ref = rope_a_ref, rope_b_ref
    k_scale = RET_QK_DIM ** -0.5
    for sub in range(RET_IN_TILE // TOKEN_TILE):
        base = sub * TOKEN_TILE
        tok = slice(base, base + TOKEN_TILE)
        xn = _rmsnorm(x_ref[tok, :], g_ref[...]).astype(BF16)
        cos = cos_ref[tok, :]
        sin = sin_ref[tok, :]
        for hd in range(RET_HEADS):
            lo = hd * RET_QK_DIM
            mid = lo + ROPE_HALF
            hi = lo + RET_QK_DIM
            qh = _dot(xn, w_ref[:, lo:hi])
            q1, q2 = qh[:, :ROPE_HALF], qh[:, ROPE_HALF:]
            q_ref[tok, lo:mid] = (q1 * cos - q2 * sin).astype(BF16)
            q_ref[tok, mid:hi] = (q1 * sin + q2 * cos).astype(BF16)
            kh = _dot(xn, w_ref[:, D_MODEL + lo:D_MODEL + hi]) * k_scale
            k1, k2 = kh[:, :ROPE_HALF], kh[:, ROPE_HALF:]
            for dst, kr in ((slice(lo, mid), k1 * cos - k2 * sin),
                            (slice(mid, hi), k1 * sin + k2 * cos)):
                k_ref[tok, dst] = kr.astype(BF16)
                for ci in range(TOKEN_TILE // RET_CHUNK):
                    rows = slice(ci * RET_CHUNK, (ci + 1) * RET_CHUNK)
                    kz_ref[base + ci * RET_CHUNK:base + (ci + 1) * RET_CHUNK, dst] = (
                        kr[rows, :] * zeta_ref[:, dst]).astype(BF16)
        for blk in range(RET_VALUE_WIDTH // RET_QK_DIM):
            cols = slice(blk * RET_QK_DIM, (blk + 1) * RET_QK_DIM)
            v_ref[tok, cols] = _dot(xn, w_ref[:, 2 * D_MODEL + blk * RET_QK_DIM:
                                              2 * D_MODEL + (blk + 1) * RET_QK_DIM]).astype(BF16)
            gate = _dot(xn, w_ref[:, 2 * D_MODEL + RET_VALUE_WIDTH + blk * RET_QK_DIM:
                                  2 * D_MODEL + RET_VALUE_WIDTH + (blk + 1) * RET_QK_DIM])
            sg_ref[tok, cols] = (gate * jax.nn.sigmoid(gate)).astype(BF16)


def _ret_in(x2d, g, rope, zeta, w_in, layer, make_rope):
    tokens = x2d.shape[0]

    def row(width):
        return pl.BlockSpec((RET_IN_TILE, width), lambda i: (i, 0))

    def out(width, dtype=BF16):
        return jax.ShapeDtypeStruct((tokens, width), dtype)

    rope_b = _resident((1, ROPE_HALF)) if make_rope else row(ROPE_HALF)
    rope_out = 2 if make_rope else 0
    return pl.pallas_call(
        functools.partial(_ret_in_kernel, make_rope=make_rope),
        grid=(tokens // RET_IN_TILE,),
        in_specs=[row(D_MODEL), _resident((1, D_MODEL)), row(ROPE_HALF), rope_b,
                  _resident((RET_CHUNK, D_MODEL)), _layer_resident((D_MODEL, RET_IN_WIDTH), layer)],
        out_specs=[row(D_MODEL), row(D_MODEL), row(D_MODEL), row(RET_VALUE_WIDTH),
                   row(RET_VALUE_WIDTH)] + [row(ROPE_HALF)] * rope_out,
        out_shape=[out(D_MODEL), out(D_MODEL), out(D_MODEL), out(RET_VALUE_WIDTH),
                   out(RET_VALUE_WIDTH)] + [out(ROPE_HALF, F32)] * rope_out,
        compiler_params=_params(1),
        name="ret_in",
    )(x2d, g, *rope, zeta, w_in)


def _ret_core_kernel(x_ref, q_ref, k_ref, kz_ref, v_ref, sg_ref, d_ref, xi_ref, w_out_ref, o_ref,
                     state_ref, y_ref):
    @pl.when(pl.program_id(1) == 0)
    def _():
        state_ref[...] = jnp.zeros_like(state_ref)

    c = RET_CHUNK
    for hd in range(RET_HEADS):
        gamma_chunk = (1.0 - 2.0 ** (-5 - hd)) ** c
        xi = jnp.concatenate([xi_ref[hd]] * (RET_V_DIM // 128), axis=-1)
        qk = slice(hd * RET_QK_DIM, (hd + 1) * RET_QK_DIM)
        vv = slice(hd * RET_V_DIM, (hd + 1) * RET_V_DIM)
        for ci in range(TOKEN_TILE // c):
            rows = slice(ci * c, (ci + 1) * c)
            qc = q_ref[0, rows, qk]
            vc = v_ref[0, rows, vv]
            state = state_ref[hd]
            inner = lax.dot_general(qc, k_ref[0, rows, qk], (((1,), (1,)), ((), ())),
                                    preferred_element_type=F32) * d_ref[hd]
            o = _dot(inner.astype(BF16), vc) + _dot(qc, state.astype(BF16)) * xi
            state_ref[hd] = state * gamma_chunk + lax.dot_general(
                kz_ref[0, rows, qk], vc, (((0,), (0,)), ((), ())), preferred_element_type=F32)
            mu = jnp.mean(o, axis=-1, keepdims=True)
            dev = o - mu
            var = jnp.mean(dev * dev, axis=-1, keepdims=True)
            y = sg_ref[0, rows, vv].astype(F32) * (dev * lax.rsqrt(var + EPS))
            y_ref[rows, vv] = y.astype(BF16)
    o_ref[0] = x_ref[0] + _dot(y_ref[...], w_out_ref[...])


def _ret_core(x3d, q, k, kz, v, sg, d_mat, xi, w_out, layer):
    bsz, seq, _ = x3d.shape

    def row(width):
        return pl.BlockSpec((1, TOKEN_TILE, width), lambda b, i: (b, i, 0))

    return pl.pallas_call(
        _ret_core_kernel,
        grid=(bsz, seq // TOKEN_TILE),
        in_specs=[row(D_MODEL), row(D_MODEL), row(D_MODEL), row(D_MODEL), row(RET_VALUE_WIDTH),
                  row(RET_VALUE_WIDTH), _resident((RET_HEADS, RET_CHUNK, RET_CHUNK)),
                  _resident((RET_HEADS, RET_CHUNK, 128)),
                  _layer_resident((RET_VALUE_WIDTH, D_MODEL), layer)],
        out_specs=row(D_MODEL),
        out_shape=jax.ShapeDtypeStruct(x3d.shape, F32),
        scratch_shapes=[pltpu.VMEM((RET_HEADS, RET_QK_DIM, RET_V_DIM), F32),
                        pltpu.VMEM((TOKEN_TILE, RET_VALUE_WIDTH), BF16)],
        compiler_params=_params(2),
        name="ret_core",
    )(x3d, q, k, kz, v, sg, d_mat, xi, w_out)


def _s5_operator_kernel(rows_ref, cols_ref, b_ref, c_ref, ct_ref, w_ref, z_ref, r_ref, a_ref, m_ref):
    n = S5_CHUNK
    lane = lax.broadcasted_iota(jnp.int32, (1, 2 * S5_STATE), 1)
    sign = jnp.where(lane < S5_STATE, -1.0, 1.0)
    lr, li = rows_ref[0, 0:1, :], rows_ref[0, 1:2, :]
    dt = jnp.exp(rows_ref[0, 2:3, :])
    mag = jnp.exp(lr * dt)
    ar = mag * jnp.cos(li * dt)
    ai = mag * jnp.sin(li * dt)
    den = lr * lr + li * li
    nr = ar - 1.0
    zr = (nr * lr + ai * li) / den
    zi = (ai * lr - nr * li) / den
    bb_ri = zr * b_ref[0, 0] + zi * b_ref[0, 1]
    bb_ir = zr * b_ref[0, 2] + zi * b_ref[0, 3]
    bb_rr = jnp.where(lane < S5_STATE, bb_ri, bb_ir)
    bb_ii = jnp.where(lane < S5_STATE, bb_ir, bb_ri)

    full_mag = jnp.exp(lr * dt * n)
    alr = full_mag * jnp.cos(li * dt * n)
    ali = full_mag * jnp.sin(li * dt * n) * sign
    row = lax.broadcasted_iota(jnp.int32, (8, 2 * S5_STATE), 0)
    a_ref[0] = jnp.where(row == 0, alr, jnp.where(row == 1, ali, jnp.where(row == 2, -ali, 0.0)))

    back = (n - 1 - lax.broadcasted_iota(jnp.int32, (n, 2 * S5_STATE), 0)).astype(F32)
    back_mag = jnp.exp(lr * dt * back)
    pw_r = back_mag * jnp.cos(li * dt * back)
    pw_i = back_mag * jnp.sin(li * dt * back)
    for hi in range(S5_GROUP):
        sl = slice(hi, hi + 1)
        rows = slice(hi * n, (hi + 1) * n)
        w_ref[0, rows, :2 * S5_STATE] = (pw_r * bb_ri[sl] + pw_i * (bb_ir[sl] * sign)).astype(BF16)
        w_ref[0, rows, 2 * S5_STATE:] = (pw_r * bb_ir[sl] - pw_i * (bb_ri[sl] * sign)).astype(BF16)
        m_ref[hi * S5_GROUP:(hi + 1) * S5_GROUP, :] = bb_rr[sl] * c_ref[0, 0] + bb_ii[sl] * c_ref[0, 1]

    fwd = (lax.broadcasted_iota(jnp.int32, (2 * S5_STATE, n), 1) + 1).astype(F32)
    fwd_mag = jnp.exp(cols_ref[0, 0] * dt * fwd)
    p1_r = fwd_mag * jnp.cos(cols_ref[0, 1] * dt * fwd)
    p1_i = fwd_mag * jnp.sin(cols_ref[0, 1] * dt * fwd)
    for ho in range(S5_GROUP):
        z_ref[0, :, ho * n:(ho + 1) * n] = (ct_ref[0, 0][:, ho:ho + 1] * p1_r
                                            + ct_ref[0, 1][:, ho:ho + 1] * p1_i).astype(BF16)
    first = lax.broadcasted_iota(jnp.int32, (2 * S5_STATE, n), 1) == 0
    p0_r = jnp.where(first, 1.0, pltpu.roll(p1_r, 1, 1))
    p0_i = jnp.where(first, 0.0, pltpu.roll(p1_i, 1, 1))
    top = lax.broadcasted_iota(jnp.int32, (2 * S5_STATE, n), 0) < S5_STATE
    r_ref[0] = jnp.dot(m_ref[...], jnp.where(top, p0_r, p0_i), preferred_element_type=F32,
                       precision=lax.Precision.HIGHEST)


def _s5_operators(lam_re, lam_im, log_dt, b_re, b_im, c_re, c_im):
    def pair(u, v):
        return jnp.concatenate([u, v], axis=-1)

    lr, li = lam_re.astype(F32), lam_im.astype(F32)
    lam = jnp.stack([pair(lr, lr), pair(li, li)], axis=1)
    ldt = jnp.broadcast_to(log_dt.astype(F32)[:, None, None], (S5_GROUPS, 1, 2 * S5_STATE))
    rows = jnp.concatenate([lam, ldt, jnp.zeros((S5_GROUPS, 5, 2 * S5_STATE), F32)], axis=1)
    cols = jnp.broadcast_to(lam[..., None], (S5_GROUPS, 2, 2 * S5_STATE, S5_CHUNK))
    br, bi = b_re.astype(F32).transpose(0, 2, 1), b_im.astype(F32).transpose(0, 2, 1)
    b = jnp.stack([pair(br, bi), pair(-bi, br), pair(bi, br), pair(br, -bi)], axis=1)
    cr, ci = c_re.astype(F32), c_im.astype(F32)
    c = jnp.stack([pair(cr, -ci), pair(-ci, -cr)], axis=1)
    ct = c.transpose(0, 1, 3, 2)

    def per_group(*shape):
        return pl.BlockSpec((1,) + shape, lambda g: (g,) + (0,) * len(shape))

    w, z, resp, a = pl.pallas_call(
        _s5_operator_kernel,
        grid=(S5_GROUPS,),
        in_specs=[per_group(8, 2 * S5_STATE), per_group(2, 2 * S5_STATE, S5_CHUNK),
                  per_group(4, S5_GROUP, 2 * S5_STATE), per_group(2, S5_GROUP, 2 * S5_STATE),
                  per_group(2, 2 * S5_STATE, S5_GROUP)],
        out_specs=[per_group(S5_COLS, 4 * S5_STATE), per_group(2 * S5_STATE, S5_COLS),
                   per_group(S5_GROUP * S5_GROUP, S5_CHUNK), per_group(8, 2 * S5_STATE)],
        out_shape=[jax.ShapeDtypeStruct((S5_GROUPS, S5_COLS, 4 * S5_STATE), BF16),
                   jax.ShapeDtypeStruct((S5_GROUPS, 2 * S5_STATE, S5_COLS), BF16),
                   jax.ShapeDtypeStruct((S5_GROUPS, S5_GROUP * S5_GROUP, S5_CHUNK), F32),
                   jax.ShapeDtypeStruct((S5_GROUPS, 8, 2 * S5_STATE), F32)],
        scratch_shapes=[pltpu.VMEM((S5_GROUP * S5_GROUP, 2 * S5_STATE), F32)],
        compiler_params=_params(1),
        name="s5_operators",
    )(rows, cols, b, c, ct)
    return resp, w, z, a


def _channel_rows(ref, ch):
    rows, channels, width = ref.shape
    return ref.reshape(rows * channels, width).at[pl.ds(ch, rows, stride=channels), :]


def _toeplitz_block(r_ref, row, causal):
    n = S5_CHUNK
    lags = jnp.broadcast_to(r_ref[0, pl.ds(row, 1), :], (n, n))
    blk = pltpu.roll(lags, 0, 1, stride=1, stride_axis=0)
    return jnp.where(causal, blk, 0.0).astype(BF16)


def _s5_core_kernel(u_ref, r0_ref, r_next_ref, w_ref, z_ref, a_ref, y_ref, t0_ref, t1_ref, us_ref,
                    v_ref, xin_ref, *, bsz, n_chunks):
    n = S5_CHUNK
    g = pl.program_id(0)
    causal = (lax.broadcasted_iota(jnp.int32, (n, n), 1)
              >= lax.broadcasted_iota(jnp.int32, (n, n), 0))

    @pl.when(g == 0)
    def _():
        def build_rows(hi, carry):
            for ho in range(S5_GROUP):
                t0_ref[pl.ds(pl.multiple_of(hi * n, n), n), ho * n:(ho + 1) * n] = (
                    _toeplitz_block(r0_ref, hi * S5_GROUP + ho, causal))
            return carry

        lax.fori_loop(0, S5_GROUP, build_rows, 0)

    @pl.when(g % 2 == 0)
    def _():
        _s5_group(u_ref, r_next_ref, w_ref, z_ref, a_ref, y_ref, t0_ref, t1_ref, us_ref, v_ref,
                  xin_ref, causal, bsz, n_chunks)

    @pl.when(g % 2 == 1)
    def _():
        _s5_group(u_ref, r_next_ref, w_ref, z_ref, a_ref, y_ref, t1_ref, t0_ref, us_ref, v_ref,
                  xin_ref, causal, bsz, n_chunks)


def _s5_group(u_ref, r_next_ref, w_ref, z_ref, a_ref, y_ref, t_ref, t_next_ref, us_ref, v_ref,
              xin_ref, causal, bsz, n_chunks):
    n = S5_CHUNK
    for hi in range(S5_GROUP):
        for ho in range(S5_GROUP):
            t_next_ref[hi * n:(hi + 1) * n, ho * n:(ho + 1) * n] = (
                _toeplitz_block(r_next_ref, hi * S5_GROUP + ho, causal))

    for hi in range(S5_GROUP):
        us_ref[:, hi * n:(hi + 1) * n] = _channel_rows(u_ref, hi)[...].astype(BF16)
    u = us_ref[...]
    half = 2 * S5_STATE
    v = _dot(u, w_ref[0])
    v_ref[0] = v[:, :half]
    v_ref[1] = v[:, half:]
    a_same = a_ref[0, 0:1, :]
    a_cross_p = a_ref[0, 1:2, :]
    a_cross_q = a_ref[0, 2:3, :]
    p = jnp.zeros((bsz, half), F32)
    q = jnp.zeros((bsz, half), F32)
    for c in range(n_chunks):
        xin_ref[pl.ds(c, bsz, stride=n_chunks), :] = p
        p, q = (a_same * p + a_cross_p * q + v_ref[0, pl.ds(c, bsz, stride=n_chunks), :],
                a_same * q + a_cross_q * p + v_ref[1, pl.ds(c, bsz, stride=n_chunks), :])
    xin = xin_ref[...].astype(BF16)
    for pair in range(S5_GROUP // 2):
        cols = slice(2 * pair * n, (2 * pair + 2) * n)
        y = _dot(u, t_ref[:, cols]) + _dot(xin, z_ref[0, :, cols])
        _channel_rows(y_ref, 2 * pair)[...] = y[:, :n]
        _channel_rows(y_ref, 2 * pair + 1)[...] = y[:, n:]


def _s5_core(u3d, resp, w, z, a, bsz):
    rows = u3d.shape[0]
    chunk_rows = pl.BlockSpec((rows, S5_GROUP, S5_CHUNK), lambda g: (0, g, 0))
    return pl.pallas_call(
        functools.partial(_s5_core_kernel, bsz=bsz, n_chunks=rows // bsz),
        grid=(S5_GROUPS,),
        in_specs=[chunk_rows,
                  pl.BlockSpec((1, S5_GROUP * S5_GROUP, S5_CHUNK), lambda g: (0, 0, 0)),
                  pl.BlockSpec((1, S5_GROUP * S5_GROUP, S5_CHUNK),
                               lambda g: (jnp.minimum(g + 1, S5_GROUPS - 1), 0, 0)),
                  pl.BlockSpec((1, S5_COLS, 4 * S5_STATE), lambda g: (g, 0, 0)),
                  pl.BlockSpec((1, 2 * S5_STATE, S5_COLS), lambda g: (g, 0, 0)),
                  pl.BlockSpec((1, 8, 2 * S5_STATE), lambda g: (g, 0, 0))],
        out_specs=chunk_rows,
        out_shape=jax.ShapeDtypeStruct((rows, D_MODEL, S5_CHUNK), F32),
        scratch_shapes=[pltpu.VMEM((S5_COLS, S5_COLS), BF16),
                        pltpu.VMEM((S5_COLS, S5_COLS), BF16),
                        pltpu.VMEM((rows, S5_COLS), BF16),
                        pltpu.VMEM((2, rows, 2 * S5_STATE), F32),
                        pltpu.VMEM((rows, 2 * S5_STATE), F32)],
        compiler_params=_params(1),
        name="s5_core",
    )(u3d, resp, resp, w, z, a)


def _gelu_tanh(x):
    return 0.5 * x * (1.0 + jnp.tanh(math.sqrt(2.0 / math.pi) * (x + 0.044715 * (x * x * x))))


def _s5_post_kernel(x_ref, y_ref, g_ref, d_ref, w_ref, o_ref, act_ref):
    per_sub = TOKEN_TILE // S5_CHUNK
    for sub in range(S5_TILE_CHUNKS // per_sub):
        tok = slice(sub * TOKEN_TILE, (sub + 1) * TOKEN_TILE)
        x = x_ref[0, tok, :]
        skip = d_ref[...] * _rmsnorm(x, g_ref[...])
        for c in range(per_sub):
            rows = slice(sub * TOKEN_TILE + c * S5_CHUNK, sub * TOKEN_TILE + (c + 1) * S5_CHUNK)
            act_ref[rows, :] = _gelu_tanh(y_ref[sub * per_sub + c].T
                                          + skip[c * S5_CHUNK:(c + 1) * S5_CHUNK, :]).astype(BF16)
        act = act_ref[tok, :]
        for blk in range(D_MODEL // GLU_BLOCK):
            cols = slice(blk * GLU_BLOCK, (blk + 1) * GLU_BLOCK)
            a = _dot(act, w_ref[:, cols])
            gate = _dot(act, w_ref[:, D_MODEL + blk * GLU_BLOCK:D_MODEL + (blk + 1) * GLU_BLOCK])
            o_ref[0, tok, cols] = x[:, cols] + a * jax.nn.sigmoid(gate)


def _s5_post(x3d, y3d, g, d_skip, w_glu, layer):
    bsz, seq, _ = x3d.shape
    tile = S5_TILE_CHUNKS * S5_CHUNK
    steps = seq // tile
    row = pl.BlockSpec((1, tile, D_MODEL), lambda b, i: (b, i, 0))
    return pl.pallas_call(
        _s5_post_kernel,
        grid=(bsz, steps),
        in_specs=[row,
                  pl.BlockSpec((S5_TILE_CHUNKS, D_MODEL, S5_CHUNK), lambda b, i: (b * steps + i, 0, 0)),
                  _resident((1, D_MODEL)), _resident((1, D_MODEL)),
                  _layer_resident((D_MODEL, 2 * D_MODEL), layer)],
        out_specs=row,
        out_shape=jax.ShapeDtypeStruct(x3d.shape, F32),
        scratch_shapes=[pltpu.VMEM((tile, D_MODEL), BF16)],
        compiler_params=_params(2),
        name="s5_post",
    )(x3d, y3d, g, d_skip, w_glu)


def kernel(x, mem, positions, norm_gains, mem_norm, final_norm, ffn1_w_in, ffn1_w_out, ffn2_w_in, ffn2_w_out, s5_lam_re, s5_lam_im, s5_log_dt, s5_b_re, s5_b_im, s5_c_re, s5_c_im, s5_d, s5_w_glu, ret_w_in, ret_w_out, xattn_w_q, xattn_w_kv, xattn_w_o):
    bsz, seq, _ = x.shape
    tokens = bsz * seq
    gains = norm_gains.astype(F32)
    gf = final_norm.astype(F32).reshape(1, D_MODEL)

    kv = _memory_kv(mem.reshape(bsz * N_MEM, D_MODEL), mem_norm.astype(F32).reshape(1, D_MODEL),
                    xattn_w_kv.astype(BF16))
    kv = kv.reshape(DEPTH, bsz, N_MEM, 2 * D_MODEL)
    pos_lanes = jnp.broadcast_to(positions.astype(F32).reshape(tokens, 1), (tokens, ROPE_HALF))
    inv_freq = (1.0 / (ROPE_BASE ** jnp.linspace(0.0, 1.0, ROPE_HALF, dtype=F32))).reshape(1, ROPE_HALF)
    d_mat, xi, zeta = _ret_decay_tables()
    rope_tables = None
    ffn1_w, ffn2_w = (ffn1_w_in.astype(BF16), ffn1_w_out.astype(BF16)), \
                     (ffn2_w_in.astype(BF16), ffn2_w_out.astype(BF16))
    ret_w_in, ret_w_out, s5_w_glu = ret_w_in.astype(BF16), ret_w_out.astype(BF16), s5_w_glu.astype(BF16)
    xattn_w_q, xattn_w_o = xattn_w_q.astype(BF16), xattn_w_o.astype(BF16)

    for i in range(DEPTH):
        g = [gains[i, j].reshape(1, D_MODEL) for j in range(4)]
        ffn1 = functools.partial(_ffn, x.reshape(tokens, D_MODEL), g[0], *ffn1_w, i)
        j = i // 2
        if i % 2 == 0:
            resp, w, z, a = _s5_operators(s5_lam_re[j], s5_lam_im[j], s5_log_dt[j], s5_b_re[j],
                                          s5_b_im[j], s5_c_re[j], s5_c_im[j])
            x2d, u = ffn1(g[1], "s5")
            x = x2d.reshape(bsz, seq, D_MODEL)
            y = _s5_core(u, resp, w, z, a, bsz)
            x = _s5_post(x, y, g[1], s5_d[j].astype(F32).reshape(1, D_MODEL), s5_w_glu, j)
        else:
            x2d, = ffn1(gf)
            if rope_tables is None:
                *heads, cos, sin = _ret_in(x2d, g[1], (pos_lanes, inv_freq), zeta, ret_w_in, j, True)
                rope_tables = (cos, sin)
            else:
                heads = _ret_in(x2d, g[1], rope_tables, zeta, ret_w_in, j, False)
            q, k, kz, v, sg = [t.reshape(bsz, seq, t.shape[-1]) for t in heads]
            x = _ret_core(x2d.reshape(bsz, seq, D_MODEL), q, k, kz, v, sg, d_mat, xi, ret_w_out, j)
        x = _xattn(x, g[2], xattn_w_q, kv, xattn_w_o, i)
        x, = _ffn(x.reshape(tokens, D_MODEL), g[3], *ffn2_w, i, gf,
                  "final" if i == DEPTH - 1 else None)
        x = x.reshape(bsz, seq, D_MODEL)
    return x
```

```python
import functools
import math

import jax
import jax.numpy as jnp
from jax import lax
from jax.experimental import pallas as pl
from jax.experimental.pallas import tpu as pltpu

F32 = jnp.float32
BF16 = jnp.bfloat16

D_MODEL = 1024
DEPTH = 4
N_MEM = 256
D_FF = 2816
MACARON_WEIGHT = 0.5
EPS = 1e-6

S5_GROUP = 16
S5_GROUPS = D_MODEL // S5_GROUP
S5_STATE = 64
S5_CHUNK = 128
S5_COLS = S5_GROUP * S5_CHUNK
S5_TILE_CHUNKS = 8

RET_HEADS = 4
RET_QK_DIM = D_MODEL // RET_HEADS
RET_V_DIM = 2 * RET_QK_DIM
RET_VALUE_WIDTH = RET_HEADS * RET_V_DIM
RET_IN_WIDTH = 2 * D_MODEL + 2 * RET_VALUE_WIDTH
RET_CHUNK = 256
ROPE_BASE = 10000.0
ROPE_HALF = RET_QK_DIM // 2

XATTN_HEADS = 4
XATTN_DIM = D_MODEL // XATTN_HEADS

TOKEN_TILE = 512
FFN_TILE = 1024
FF_BLOCK = 256
GLU_BLOCK = 256
XATTN_TILE = 2048
RET_IN_TILE = 1024
VMEM_LIMIT = 56 * 1024 * 1024


def _params(n_axes):
    return pltpu.CompilerParams(dimension_semantics=("arbitrary",) * n_axes,
                                vmem_limit_bytes=VMEM_LIMIT)


def _resident(shape):
    zeros = (0,) * len(shape)
    return pl.BlockSpec(shape, lambda *_: zeros, pipeline_mode=pl.Buffered(1))


def _layer_resident(shape, layer):
    zeros = (0,) * len(shape)
    return pl.BlockSpec((None,) + shape, lambda *_: (layer,) + zeros, pipeline_mode=pl.Buffered(1))


def _rmsnorm(x, g):
    return x * lax.rsqrt(jnp.mean(x * x, axis=-1, keepdims=True) + EPS) * g


def _dot(a, b):
    return jnp.dot(a, b, preferred_element_type=F32)


def _chunk_slab(ref, chunk):
    channels, chunks, width = ref.shape
    return ref.reshape(channels * chunks, width).at[pl.ds(chunk, channels, stride=chunks), :]


def _ffn_kernel(x_ref, g_ref, w_in_ref, w_out_ref, gn_ref, o_ref, *extra_refs, extra):
    *extra_refs, act_ref = extra_refs
    per_sub = TOKEN_TILE // S5_CHUNK
    for sub in range(FFN_TILE // TOKEN_TILE):
        tok = slice(sub * TOKEN_TILE, (sub + 1) * TOKEN_TILE)
        x = x_ref[tok, :]
        xn = _rmsnorm(x, g_ref[...]).astype(BF16)
        for j in range(D_FF // FF_BLOCK):
            cols = slice(j * FF_BLOCK, (j + 1) * FF_BLOCK)
            a = _dot(xn, w_in_ref[:, cols])
            b = _dot(xn, w_in_ref[:, D_FF + j * FF_BLOCK:D_FF + (j + 1) * FF_BLOCK])
            act_ref[tok, cols] = (a * jax.nn.sigmoid(a) * b).astype(BF16)
        y = x + MACARON_WEIGHT * _dot(act_ref[tok, :], w_out_ref[...])
        if extra == "final":
            y = _rmsnorm(y, gn_ref[...])
        o_ref[tok, :] = y
        if extra == "s5":
            u_ref, = extra_refs
            yn = _rmsnorm(y, gn_ref[...])
            for c in range(per_sub):
                _chunk_slab(u_ref, sub * per_sub + c)[...] = yn[c * S5_CHUNK:(c + 1) * S5_CHUNK, :].T


def _ffn(x2d, g, w_in, w_out, layer, gn, extra=None):
    tokens = x2d.shape[0]
    row = pl.BlockSpec((FFN_TILE, D_MODEL), lambda i: (i, 0))
    out_specs = [row]
    out_shape = [jax.ShapeDtypeStruct((tokens, D_MODEL), F32)]
    if extra == "s5":
        per_tile = FFN_TILE // S5_CHUNK
        out_specs.append(pl.BlockSpec((D_MODEL, per_tile, S5_CHUNK), lambda i: (0, i, 0)))
        out_shape.append(jax.ShapeDtypeStruct((D_MODEL, tokens // S5_CHUNK, S5_CHUNK), F32))
    return pl.pallas_call(
        functools.partial(_ffn_kernel, extra=extra),
        grid=(tokens // FFN_TILE,),
        in_specs=[row, _resident((1, D_MODEL)), _layer_resident((D_MODEL, 2 * D_FF), layer),
                  _layer_resident((D_FF, D_MODEL), layer), _resident((1, D_MODEL))],
        out_specs=out_specs,
        out_shape=out_shape,
        scratch_shapes=[pltpu.VMEM((FFN_TILE, D_FF), BF16)],
        compiler_params=_params(1),
        name="ffn",
    )(x2d, g, w_in, w_out, gn)


def _kv_kernel(mem_ref, g_ref, w_ref, o_ref):
    mn = _rmsnorm(mem_ref[...], g_ref[...]).astype(BF16)
    o_ref[0] = _dot(mn, w_ref[0]).astype(BF16)


def _memory_kv(mem2d, g, w_kv):
    rows = mem2d.shape[0]
    half = rows // 2
    return pl.pallas_call(
        _kv_kernel,
        grid=(DEPTH, 2),
        in_specs=[pl.BlockSpec((half, D_MODEL), lambda l, r: (r, 0)),
                  _resident((1, D_MODEL)),
                  pl.BlockSpec((1, D_MODEL, 2 * D_MODEL), lambda l, r: (l, 0, 0))],
        out_specs=pl.BlockSpec((1, half, 2 * D_MODEL), lambda l, r: (l, r, 0)),
        out_shape=jax.ShapeDtypeStruct((DEPTH, rows, 2 * D_MODEL), BF16),
        compiler_params=_params(2),
        name="memory_kv",
    )(mem2d, g, w_kv)


def _xattn_kernel(x_ref, g_ref, wq_ref, k_ref, v_ref, wo_ref, o_ref):
    for sub in range(XATTN_TILE // TOKEN_TILE):
        rows = slice(sub * TOKEN_TILE, (sub + 1) * TOKEN_TILE)
        x = x_ref[0, rows, :]
        xn = _rmsnorm(x, g_ref[...]).astype(BF16)
        q = (_dot(xn, wq_ref[...]) * (XATTN_DIM ** -0.5)).astype(BF16)
        heads = []
        for h in range(XATTN_HEADS):
            sl = slice(h * XATTN_DIM, (h + 1) * XATTN_DIM)
            s = lax.dot_general(q[:, sl], k_ref[0, :, sl], (((1,), (1,)), ((), ())),
                                preferred_element_type=F32)
            e = jnp.exp(s - jnp.max(s, axis=-1, keepdims=True))
            p = e / jnp.sum(e, axis=-1, keepdims=True)
            heads.append(_dot(p.astype(BF16), v_ref[0, :, sl]).astype(BF16))
        o = jnp.concatenate(heads, axis=-1)
        o_ref[0, rows, :] = x + _dot(o, wo_ref[...])


def _xattn(x3d, g, w_q, kv, w_o, layer):
    bsz, seq, _ = x3d.shape
    row = pl.BlockSpec((1, XATTN_TILE, D_MODEL), lambda b, i: (b, i, 0))
    return pl.pallas_call(
        _xattn_kernel,
        grid=(bsz, seq // XATTN_TILE),
        in_specs=[row, _resident((1, D_MODEL)), _layer_resident((D_MODEL, D_MODEL), layer),
                  pl.BlockSpec((None, 1, N_MEM, D_MODEL), lambda b, i: (layer, b, 0, 0)),
                  pl.BlockSpec((None, 1, N_MEM, D_MODEL), lambda b, i: (layer, b, 0, 1)),
                  _layer_resident((D_MODEL, D_MODEL), layer)],
        out_specs=row,
        out_shape=jax.ShapeDtypeStruct(x3d.shape, F32),
        compiler_params=_params(2),
        name="xattn",
    )(x3d, g, w_q, kv, kv, w_o)


def _ret_decay_tables():
    c = RET_CHUNK
    log_gamma = jnp.log(1.0 - jnp.exp2(-5.0 - jnp.arange(RET_HEADS, dtype=F32)))
    idx = jnp.arange(c)
    diff = idx[:, None] - idx[None, :]
    d_mat = jnp.where(diff >= 0, jnp.exp(log_gamma[:, None, None]
                                         * jnp.maximum(diff, 0).astype(F32)), 0.0)
    xi = jnp.exp(log_gamma[:, None] * (idx + 1).astype(F32))
    zeta = jnp.exp(log_gamma[:, None] * (c - 1 - idx).astype(F32))
    xi = jnp.broadcast_to(xi[:, :, None], (RET_HEADS, c, 128))
    zeta = jnp.broadcast_to(zeta.T[:, :, None], (c, RET_HEADS, RET_QK_DIM)).reshape(c, D_MODEL)
    return d_mat, xi, zeta


def _ret_in_kernel(x_ref, g_ref, rope_a_ref, rope_b_ref, zeta_ref, w_ref,
                   q_ref, k_ref, kz_ref, v_ref, sg_ref, *rope_out_refs, make_rope):
    if make_rope:
        cos_ref, sin_ref = rope_out_refs
        ang = rope_a_ref[...] * rope_b_ref[...]
        cos_ref[...] = jnp.cos(ang)
        sin_ref[...] = jnp.sin(ang)
    else:
        cos_ref, sin_ref = rope_a_ref, rope_b_ref
    k_scale = RET_QK_DIM ** -0.5
    for sub in range(RET_IN_TILE // TOKEN_TILE):
        base = sub * TOKEN_TILE
        tok = slice(base, base + TOKEN_TILE)
        xn = _rmsnorm(x_ref[tok, :], g_ref[...]).astype(BF16)
        cos = cos_ref[tok, :]
        sin = sin_ref[tok, :]
        for hd in range(RET_HEADS):
            lo = hd * RET_QK_DIM
            mid = lo + ROPE_HALF
            hi = lo + RET_QK_DIM
            qh = _dot(xn, w_ref[:, lo:hi])
            q1, q2 = qh[:, :ROPE_HALF], qh[:, ROPE_HALF:]
            q_ref[tok, lo:mid] = (q1 * cos - q2 * sin).astype(BF16)
            q_ref[tok, mid:hi] = (q1 * sin + q2 * cos).astype(BF16)
            kh = _dot(xn, w_ref[:, D_MODEL + lo:D_MODEL + hi]) * k_scale
            k1, k2 = kh[:, :ROPE_HALF], kh[:, ROPE_HALF:]
            for dst, kr in ((slice(lo, mid), k1 * cos - k2 * sin),
                            (slice(mid, hi), k1 * sin + k2 * cos)):
                k_ref[tok, dst] = kr.astype(BF16)
                for ci in range(TOKEN_TILE // RET_CHUNK):
                    rows = slice(ci * RET_CHUNK, (ci + 1) * RET_CHUNK)
                    kz_ref[base + ci * RET_CHUNK:base + (ci + 1) * RET_CHUNK, dst] = (
                        kr[rows, :] * zeta_ref[:, dst]).astype(BF16)
        for blk in range(RET_VALUE_WIDTH // RET_QK_DIM):
            cols = slice(blk * RET_QK_DIM, (blk + 1) * RET_QK_DIM)
            v_ref[tok, cols] = _dot(xn, w_ref[:, 2 * D_MODEL + blk * RET_QK_DIM:
                                              2 * D_MODEL + (blk + 1) * RET_QK_DIM]).astype(BF16)
            gate = _dot(xn, w_ref[:, 2 * D_MODEL + RET_VALUE_WIDTH + blk * RET_QK_DIM:
                                  2 * D_MODEL + RET_VALUE_WIDTH + (blk + 1) * RET_QK_DIM])
            sg_ref[tok, cols] = (gate * jax.nn.sigmoid(gate)).astype(BF16)


def _ret_in(x2d, g, rope, zeta, w_in, layer, make_rope):
    tokens = x2d.shape[0]

    def row(width):
        return pl.BlockSpec((RET_IN_TILE, width), lambda i: (i, 0))

    def out(width, dtype=BF16):
        return jax.ShapeDtypeStruct((tokens, width), dtype)

    rope_b = _resident((1, ROPE_HALF)) if make_rope else row(ROPE_HALF)
    rope_out = 2 if make_rope else 0
    return pl.pallas_call(
        functools.partial(_ret_in_kernel, make_rope=make_rope),
        grid=(tokens // RET_IN_TILE,),
        in_specs=[row(D_MODEL), _resident((1, D_MODEL)), row(ROPE_HALF), rope_b,
                  _resident((RET_CHUNK, D_MODEL)), _layer_resident((D_MODEL, RET_IN_WIDTH), layer)],
        out_specs=[row(D_MODEL), row(D_MODEL), row(D_MODEL), row(RET_VALUE_WIDTH),
                   row(RET_VALUE_WIDTH)] + [row(ROPE_HALF)] * rope_out,
        out_shape=[out(D_MODEL), out(D_MODEL), out(D_MODEL), out(RET_VALUE_WIDTH),
                   out(RET_VALUE_WIDTH)] + [out(ROPE_HALF, F32)] * rope_out,
        compiler_params=_params(1),
        name="ret_in",
    )(x2d, g, *rope, zeta, w_in)


def _ret_core_kernel(x_ref, q_ref, k_ref, kz_ref, v_ref, sg_ref, d_ref, xi_ref, w_out_ref, o_ref,
                     state_ref, y_ref):
    @pl.when(pl.program_id(1) == 0)
    def _():
        state_ref[...] = jnp.zeros_like(state_ref)

    c = RET_CHUNK
    for hd in range(RET_HEADS):
        gamma_chunk = (1.0 - 2.0 ** (-5 - hd)) ** c
        xi = jnp.concatenate([xi_ref[hd]] * (RET_V_DIM // 128), axis=-1)
        qk = slice(hd * RET_QK_DIM, (hd + 1) * RET_QK_DIM)
        vv = slice(hd * RET_V_DIM, (hd + 1) * RET_V_DIM)
        for ci in range(TOKEN_TILE // c):
            rows = slice(ci * c, (ci + 1) * c)
            qc = q_ref[0, rows, qk]
            vc = v_ref[0, rows, vv]
            state = state_ref[hd]
            inner = lax.dot_general(qc, k_ref[0, rows, qk], (((1,), (1,)), ((), ())),
                                    preferred_element_type=F32) * d_ref[hd]
            o = _dot(inner.astype(BF16), vc) + _dot(qc, state.astype(BF16)) * xi
            state_ref[hd] = state * gamma_chunk + lax.dot_general(
                kz_ref[0, rows, qk], vc, (((0,), (0,)), ((), ())), preferred_element_type=F32)
            mu = jnp.mean(o, axis=-1, keepdims=True)
            dev = o - mu
            var = jnp.mean(dev * dev, axis=-1, keepdims=True)
            y = sg_ref[0, rows, vv].astype(F32) * (dev * lax.rsqrt(var + EPS))
            y_ref[rows, vv] = y.astype(BF16)
    o_ref[0] = x_ref[0] + _dot(y_ref[...], w_out_ref[...])


def _ret_core(x3d, q, k, kz, v, sg, d_mat, xi, w_out, layer):
    bsz, seq, _ = x3d.shape

    def row(width):
        return pl.BlockSpec((1, TOKEN_TILE, width), lambda b, i: (b, i, 0))

    return pl.pallas_call(
        _ret_core_kernel,
        grid=(bsz, seq // TOKEN_TILE),
        in_specs=[row(D_MODEL), row(D_MODEL), row(D_MODEL), row(D_MODEL), row(RET_VALUE_WIDTH),
                  row(RET_VALUE_WIDTH), _resident((RET_HEADS, RET_CHUNK, RET_CHUNK)),
                  _resident((RET_HEADS, RET_CHUNK, 128)),
                  _layer_resident((RET_VALUE_WIDTH, D_MODEL), layer)],
        out_specs=row(D_MODEL),
        out_shape=jax.ShapeDtypeStruct(x3d.shape, F32),
        scratch_shapes=[pltpu.VMEM((RET_HEADS, RET_QK_DIM, RET_V_DIM), F32),
                        pltpu.VMEM((TOKEN_TILE, RET_VALUE_WIDTH), BF16)],
        compiler_params=_params(2),
        name="ret_core",
    )(x3d, q, k, kz, v, sg, d_mat, xi, w_out)


def _s5_operator_kernel(rows_ref, cols_ref, b_ref, c_ref, ct_ref, w_ref, z_ref, r_ref, a_ref, m_ref):
    n = S5_CHUNK
    lane = lax.broadcasted_iota(jnp.int32, (1, 2 * S5_STATE), 1)
    sign = jnp.where(lane < S5_STATE, -1.0, 1.0)
    lr, li = rows_ref[0, 0:1, :], rows_ref[0, 1:2, :]
    dt = jnp.exp(rows_ref[0, 2:3, :])
    mag = jnp.exp(lr * dt)
    ar = mag * jnp.cos(li * dt)
    ai = mag * jnp.sin(li * dt)
    den = lr * lr + li * li
    nr = ar - 1.0
    zr = (nr * lr + ai * li) / den
    zi = (ai * lr - nr * li) / den
    bb_ri = zr * b_ref[0, 0] + zi * b_ref[0, 1]
    bb_ir = zr * b_ref[0, 2] + zi * b_ref[0, 3]
    bb_rr = jnp.where(lane < S5_STATE, bb_ri, bb_ir)
    bb_ii = jnp.where(lane < S5_STATE, bb_ir, bb_ri)

    full_mag = jnp.exp(lr * dt * n)
    alr = full_mag * jnp.cos(li * dt * n)
    ali = full_mag * jnp.sin(li * dt * n) * sign
    row = lax.broadcasted_iota(jnp.int32, (8, 2 * S5_STATE), 0)
    a_ref[0] = jnp.where(row == 0, alr, jnp.where(row == 1, ali, jnp.where(row == 2, -ali, 0.0)))

    back = (n - 1 - lax.broadcasted_iota(jnp.int32, (n, 2 * S5_STATE), 0)).astype(F32)
    back_mag = jnp.exp(lr * dt * back)
    pw_r = back_mag * jnp.cos(li * dt * back)
    pw_i = back_mag * jnp.sin(li * dt * back)
    for hi in range(S5_GROUP):
        sl = slice(hi, hi + 1)
        rows = slice(hi * n, (hi + 1) * n)
        w_ref[0, rows, :2 * S5_STATE] = (pw_r * bb_ri[sl] + pw_i * (bb_ir[sl] * sign)).astype(BF16)
        w_ref[0, rows, 2 * S5_STATE:] = (pw_r * bb_ir[sl] - pw_i * (bb_ri[sl] * sign)).astype(BF16)
        m_ref[hi * S5_GROUP:(hi + 1) * S5_GROUP, :] = bb_rr[sl] * c_ref[0, 0] + bb_ii[sl] * c_ref[0, 1]

    fwd = (lax.broadcasted_iota(jnp.int32, (2 * S5_STATE, n), 1) + 1).astype(F32)
    fwd_mag = jnp.exp(cols_ref[0, 0] * dt * fwd)
    p1_r = fwd_mag * jnp.cos(cols_ref[0, 1] * dt * fwd)
    p1_i = fwd_mag * jnp.sin(cols_ref[0, 1] * dt * fwd)
    for ho in range(S5_GROUP):
        z_ref[0, :, ho * n:(ho + 1) * n] = (ct_ref[0, 0][:, ho:ho + 1] * p1_r
                                            + ct_ref[0, 1][:, ho:ho + 1] * p1_i).astype(BF16)
    first = lax.broadcasted_iota(jnp.int32, (2 * S5_STATE, n), 1) == 0
    p0_r = jnp.where(first, 1.0, pltpu.roll(p1_r, 1, 1))
    p0_i = jnp.where(first, 0.0, pltpu.roll(p1_i, 1, 1))
    top = lax.broadcasted_iota(jnp.int32, (2 * S5_STATE, n), 0) < S5_STATE
    r_ref[0] = jnp.dot(m_ref[...], jnp.where(top, p0_r, p0_i), preferred_element_type=F32,
                       precision=lax.Precision.HIGHEST)


def _s5_operators(lam_re, lam_im, log_dt, b_re, b_im, c_re, c_im):
    def pair(u, v):
        return jnp.concatenate([u, v], axis=-1)

    lr, li = lam_re.astype(F32), lam_im.astype(F32)
    lam = jnp.stack([pair(lr, lr), pair(li, li)], axis=1)
    ldt = jnp.broadcast_to(log_dt.astype(F32)[:, None, None], (S5_GROUPS, 1, 2 * S5_STATE))
    rows = jnp.concatenate([lam, ldt, jnp.zeros((S5_GROUPS, 5, 2 * S5_STATE), F32)], axis=1)
    cols = jnp.broadcast_to(lam[..., None], (S5_GROUPS, 2, 2 * S5_STATE, S5_CHUNK))
    br, bi = b_re.astype(F32).transpose(0, 2, 1), b_im.astype(F32).transpose(0, 2, 1)
    b = jnp.stack([pair(br, bi), pair(-bi, br), pair(bi, br), pair(br, -bi)], axis=1)
    cr, ci = c_re.astype(F32), c_im.astype(F32)
    c = jnp.stack([pair(cr, -ci), pair(-ci, -cr)], axis=1)
    ct = c.transpose(0, 1, 3, 2)

    def per_group(*shape):
        return pl.BlockSpec((1,) + shape, lambda g: (g,) + (0,) * len(shape))

    w, z, resp, a = pl.pallas_call(
        _s5_operator_kernel,
        grid=(S5_GROUPS,),
        in_specs=[per_group(8, 2 * S5_STATE), per_group(2, 2 * S5_STATE, S5_CHUNK),
                  per_group(4, S5_GROUP, 2 * S5_STATE), per_group(2, S5_GROUP, 2 * S5_STATE),
                  per_group(2, 2 * S5_STATE, S5_GROUP)],
        out_specs=[per_group(S5_COLS, 4 * S5_STATE), per_group(2 * S5_STATE, S5_COLS),
                   per_group(S5_GROUP * S5_GROUP, S5_CHUNK), per_group(8, 2 * S5_STATE)],
        out_shape=[jax.ShapeDtypeStruct((S5_GROUPS, S5_COLS, 4 * S5_STATE), BF16),
                   jax.ShapeDtypeStruct((S5_GROUPS, 2 * S5_STATE, S5_COLS), BF16),
                   jax.ShapeDtypeStruct((S5_GROUPS, S5_GROUP * S5_GROUP, S5_CHUNK), F32),
                   jax.ShapeDtypeStruct((S5_GROUPS, 8, 2 * S5_STATE), F32)],
        scratch_shapes=[pltpu.VMEM((S5_GROUP * S5_GROUP, 2 * S5_STATE), F32)],
        compiler_params=_params(1),
        name="s5_operators",
    )(rows, cols, b, c, ct)
    return resp, w, z, a


def _toeplitz_block(r_ref, row, causal):
    n = S5_CHUNK
    lags = jnp.broadcast_to(r_ref[0, pl.ds(row, 1), :], (n, n))
    blk = pltpu.roll(lags, 0, 1, stride=1, stride_axis=0)
    return jnp.where(causal, blk, 0.0).astype(BF16)


def _s5_core_kernel(u_ref, r0_ref, r_next_ref, w_ref, z_ref, a_ref, y_ref, t0_ref, t1_ref, us_ref,
                    v_ref, xin_ref, *, bsz, n_chunks):
    n = S5_CHUNK
    g = pl.program_id(0)
    causal = (lax.broadcasted_iota(jnp.int32, (n, n), 1)
              >= lax.broadcasted_iota(jnp.int32, (n, n), 0))

    @pl.when(g == 0)
    def _():
        def build_rows(hi, carry):
            for ho in range(S5_GROUP):
                t0_ref[pl.ds(pl.multiple_of(hi * n, n), n), ho * n:(ho + 1) * n] = (
                    _toeplitz_block(r0_ref, hi * S5_GROUP + ho, causal))
            return carry

        lax.fori_loop(0, S5_GROUP, build_rows, 0)

    @pl.when(g % 2 == 0)
    def _():
        _s5_group(u_ref, r_next_ref, w_ref, z_ref, a_ref, y_ref, t0_ref, t1_ref, us_ref, v_ref,
                  xin_ref, causal, bsz, n_chunks)

    @pl.when(g % 2 == 1)
    def _():
        _s5_group(u_ref, r_next_ref, w_ref, z_ref, a_ref, y_ref, t1_ref, t0_ref, us_ref, v_ref,
                  xin_ref, causal, bsz, n_chunks)


def _s5_group(u_ref, r_next_ref, w_ref, z_ref, a_ref, y_ref, t_ref, t_next_ref, us_ref, v_ref,
              xin_ref, causal, bsz, n_chunks):
    n = S5_CHUNK
    for hi in range(S5_GROUP):
        for ho in range(S5_GROUP):
            t_next_ref[hi * n:(hi + 1) * n, ho * n:(ho + 1) * n] = (
                _toeplitz_block(r_next_ref, hi * S5_GROUP + ho, causal))

    for hi in range(S5_GROUP):
        us_ref[:, hi * n:(hi + 1) * n] = u_ref[hi].astype(BF16)
    u = us_ref[...]
    half = 2 * S5_STATE
    v = _dot(u, w_ref[0])
    v_ref[0] = v[:, :half]
    v_ref[1] = v[:, half:]
    a_same = a_ref[0, 0:1, :]
    a_cross_p = a_ref[0, 1:2, :]
    a_cross_q = a_ref[0, 2:3, :]
    p = jnp.zeros((bsz, half), F32)
    q = jnp.zeros((bsz, half), F32)
    for c in range(n_chunks):
        xin_ref[pl.ds(c, bsz, stride=n_chunks), :] = p
        p, q = (a_same * p + a_cross_p * q + v_ref[0, pl.ds(c, bsz, stride=n_chunks), :],
                a_same * q + a_cross_q * p + v_ref[1, pl.ds(c, bsz, stride=n_chunks), :])
    xin = xin_ref[...].astype(BF16)
    for pair in range(S5_GROUP // 2):
        cols = slice(2 * pair * n, (2 * pair + 2) * n)
        y = _dot(u, t_ref[:, cols]) + _dot(xin, z_ref[0, :, cols])
        y_ref[2 * pair] = y[:, :n]
        y_ref[2 * pair + 1] = y[:, n:]


def _s5_core(u3d, resp, w, z, a, bsz):
    rows = u3d.shape[1]
    chunk_rows = pl.BlockSpec((S5_GROUP, rows, S5_CHUNK), lambda g: (g, 0, 0))
    return pl.pallas_call(
        functools.partial(_s5_core_kernel, bsz=bsz, n_chunks=rows // bsz),
        grid=(S5_GROUPS,),
        in_specs=[chunk_rows,
                  pl.BlockSpec((1, S5_GROUP * S5_GROUP, S5_CHUNK), lambda g: (0, 0, 0)),
                  pl.BlockSpec((1, S5_GROUP * S5_GROUP, S5_CHUNK),
                               lambda g: (jnp.minimum(g + 1, S5_GROUPS - 1), 0, 0)),
                  pl.BlockSpec((1, S5_COLS, 4 * S5_STATE), lambda g: (g, 0, 0)),
                  pl.BlockSpec((1, 2 * S5_STATE, S5_COLS), lambda g: (g, 0, 0)),
                  pl.BlockSpec((1, 8, 2 * S5_STATE), lambda g: (g, 0, 0))],
        out_specs=chunk_rows,
        out_shape=jax.ShapeDtypeStruct((D_MODEL, rows, S5_CHUNK), F32),
        scratch_shapes=[pltpu.VMEM((S5_COLS, S5_COLS), BF16),
                        pltpu.VMEM((S5_COLS, S5_COLS), BF16),
                        pltpu.VMEM((rows, S5_COLS), BF16),
                        pltpu.VMEM((2, rows, 2 * S5_STATE), F32),
                        pltpu.VMEM((rows, 2 * S5_STATE), F32)],
        compiler_params=_params(1),
        name="s5_core",
    )(u3d, resp, resp, w, z, a)


def _gelu_tanh(x):
    return 0.5 * x * (1.0 + jnp.tanh(math.sqrt(2.0 / math.pi) * (x + 0.044715 * (x * x * x))))


def _s5_post_kernel(x_ref, y_ref, g_ref, d_ref, w_ref, o_ref, act_ref):
    per_sub = TOKEN_TILE // S5_CHUNK
    for sub in range(S5_TILE_CHUNKS // per_sub):
        tok = slice(sub * TOKEN_TILE, (sub + 1) * TOKEN_TILE)
        x = x_ref[0, tok, :]
        skip = d_ref[...] * _rmsnorm(x, g_ref[...])
        for c in range(per_sub):
            rows = slice(sub * TOKEN_TILE + c * S5_CHUNK, sub * TOKEN_TILE + (c + 1) * S5_CHUNK)
            act_ref[rows, :] = _gelu_tanh(_chunk_slab(y_ref, sub * per_sub + c)[...].T
                                          + skip[c * S5_CHUNK:(c + 1) * S5_CHUNK, :]).astype(BF16)
        act = act_ref[tok, :]
        for blk in range(D_MODEL // GLU_BLOCK):
            cols = slice(blk * GLU_BLOCK, (blk + 1) * GLU_BLOCK)
            a = _dot(act, w_ref[:, cols])
            gate = _dot(act, w_ref[:, D_MODEL + blk * GLU_BLOCK:D_MODEL + (blk + 1) * GLU_BLOCK])
            o_ref[0, tok, cols] = x[:, cols] + a * jax.nn.sigmoid(gate)


def _s5_post(x3d, y3d, g, d_skip, w_glu, layer):
    bsz, seq, _ = x3d.shape
    tile = S5_TILE_CHUNKS * S5_CHUNK
    steps = seq // tile
    row = pl.BlockSpec((1, tile, D_MODEL), lambda b, i: (b, i, 0))
    return pl.pallas_call(
        _s5_post_kernel,
        grid=(bsz, steps),
        in_specs=[row,
                  pl.BlockSpec((D_MODEL, S5_TILE_CHUNKS, S5_CHUNK), lambda b, i: (0, b * steps + i, 0)),
                  _resident((1, D_MODEL)), _resident((1, D_MODEL)),
                  _layer_resident((D_MODEL, 2 * D_MODEL), layer)],
        out_specs=row,
        out_shape=jax.ShapeDtypeStruct(x3d.shape, F32),
        scratch_shapes=[pltpu.VMEM((tile, D_MODEL), BF16)],
        compiler_params=_params(2),
        name="s5_post",
    )(x3d, y3d, g, d_skip, w_glu)


def kernel(x, mem, positions, norm_gains, mem_norm, final_norm, ffn1_w_in, ffn1_w_out, ffn2_w_in, ffn2_w_out, s5_lam_re, s5_lam_im, s5_log_dt, s5_b_re, s5_b_im, s5_c_re, s5_c_im, s5_d, s5_w_glu, ret_w_in, ret_w_out, xattn_w_q, xattn_w_kv, xattn_w_o):
    bsz, seq, _ = x.shape
    tokens = bsz * seq
    gains = norm_gains.astype(F32)
    gf = final_norm.astype(F32).reshape(1, D_MODEL)

    kv = _memory_kv(mem.reshape(bsz * N_MEM, D_MODEL), mem_norm.astype(F32).reshape(1, D_MODEL),
                    xattn_w_kv.astype(BF16))
    kv = kv.reshape(DEPTH, bsz, N_MEM, 2 * D_MODEL)
    pos_lanes = jnp.broadcast_to(positions.astype(F32).reshape(tokens, 1), (tokens, ROPE_HALF))
    inv_freq = (1.0 / (ROPE_BASE ** jnp.linspace(0.0, 1.0, ROPE_HALF, dtype=F32))).reshape(1, ROPE_HALF)
    d_mat, xi, zeta = _ret_decay_tables()
    rope_tables = None
    ffn1_w, ffn2_w = (ffn1_w_in.astype(BF16), ffn1_w_out.astype(BF16)), \
                     (ffn2_w_in.astype(BF16), ffn2_w_out.astype(BF16))
    ret_w_in, ret_w_out, s5_w_glu = ret_w_in.astype(BF16), ret_w_out.astype(BF16), s5_w_glu.astype(BF16)
    xattn_w_q, xattn_w_o = xattn_w_q.astype(BF16), xattn_w_o.astype(BF16)

    for i in range(DEPTH):
        g = [gains[i, j].reshape(1, D_MODEL) for j in range(4)]
        ffn1 = functools.partial(_ffn, x.reshape(tokens, D_MODEL), g[0], *ffn1_w, i)
        j = i // 2
        if i % 2 == 0:
            resp, w, z, a = _s5_operators(s5_lam_re[j], s5_lam_im[j], s5_log_dt[j], s5_b_re[j],
                                          s5_b_im[j], s5_c_re[j], s5_c_im[j])
            x2d, u = ffn1(g[1], "s5")
            x = x2d.reshape(bsz, seq, D_MODEL)
            y = _s5_core(u, resp, w, z, a, bsz)
            x = _s5_post(x, y, g[1], s5_d[j].astype(F32).reshape(1, D_MODEL), s5_w_glu, j)
        else:
            x2d, = ffn1(gf)
            if rope_tables is None:
                *heads, cos, sin = _ret_in(x2d, g[1], (pos_lanes, inv_freq), zeta, ret_w_in, j, True)
                rope_tables = (cos, sin)
            else:
                heads = _ret_in(x2d, g[1], rope_tables, zeta, ret_w_in, j, False)
            q, k, kz, v, sg = [t.reshape(bsz, seq, t.shape[-1]) for t in heads]
            x = _ret_core(x2d.reshape(bsz, seq, D_MODEL), q, k, kz, v, sg, d_mat, xi, ret_w_out, j)
        x = _xattn(x, g[2], xattn_w_q, kv, xattn_w_o, i)
        x, = _ffn(x.reshape(tokens, D_MODEL), g[3], *ffn2_w, i, gf,
                  "final" if i == DEPTH - 1 else None)
        x = x.reshape(bsz, seq, D_MODEL)
    return x
```

```python
import functools
import math

import jax
import jax.numpy as jnp
from jax import lax
from jax.experimental import pallas as pl
from jax.experimental.pallas import tpu as pltpu

F32 = jnp.float32
BF16 = jnp.bfloat16

D_MODEL = 1024
DEPTH = 4
N_MEM = 256
D_FF = 2816
MACARON_WEIGHT = 0.5
EPS = 1e-6

S5_GROUP = 16
S5_GROUPS = D_MODEL // S5_GROUP
S5_STATE = 64
S5_CHUNK = 128
S5_COLS = S5_GROUP * S5_CHUNK
S5_TILE_CHUNKS = 8

RET_HEADS = 4
RET_QK_DIM = D_MODEL // RET_HEADS
RET_V_DIM = 2 * RET_QK_DIM
RET_VALUE_WIDTH = RET_HEADS * RET_V_DIM
RET_IN_WIDTH = 2 * D_MODEL + 2 * RET_VALUE_WIDTH
RET_CHUNK = 256
ROPE_BASE = 10000.0
ROPE_HALF = RET_QK_DIM // 2

XATTN_HEADS = 4
XATTN_DIM = D_MODEL // XATTN_HEADS

TOKEN_TILE = 512
FFN_TILE = 1024
FF_BLOCK = 256
GLU_BLOCK = 256
XATTN_TILE = 2048
RET_IN_TILE = 1024
VMEM_LIMIT = 56 * 1024 * 1024
LANES = 128


def _params(n_axes):
    return pltpu.CompilerParams(dimension_semantics=("arbitrary",) * n_axes,
                                vmem_limit_bytes=VMEM_LIMIT)


def _resident(shape):
    zeros = (0,) * len(shape)
    return pl.BlockSpec(shape, lambda *_: zeros, pipeline_mode=pl.Buffered(1))


def _layer_resident(shape, layer):
    zeros = (0,) * len(shape)
    return pl.BlockSpec((None,) + shape, lambda *_: (layer,) + zeros, pipeline_mode=pl.Buffered(1))


def _rmsnorm(x, g):
    return x * lax.rsqrt(jnp.mean(x * x, axis=-1, keepdims=True) + EPS) * g


def _dot(a, b):
    return jnp.dot(a, b, preferred_element_type=F32)


def _chunk_slab(ref, chunk):
    channels, chunks, width = ref.shape
    return ref.reshape(channels * chunks, width).at[pl.ds(chunk, channels, stride=chunks), :]


def _ffn_kernel(x_ref, g_ref, w_in_ref, w_out_ref, gn_ref, o_ref, *extra_refs, extra):
    *extra_refs, act_ref = extra_refs
    per_sub = TOKEN_TILE // S5_CHUNK
    for sub in range(FFN_TILE // TOKEN_TILE):
        tok = slice(sub * TOKEN_TILE, (sub + 1) * TOKEN_TILE)
        x = x_ref[tok, :]
        xn = _rmsnorm(x, g_ref[...]).astype(BF16)
        for j in range(D_FF // FF_BLOCK):
            cols = slice(j * FF_BLOCK, (j + 1) * FF_BLOCK)
            a = _dot(xn, w_in_ref[:, cols])
            b = _dot(xn, w_in_ref[:, D_FF + j * FF_BLOCK:D_FF + (j + 1) * FF_BLOCK])
            act_ref[tok, cols] = (a * jax.nn.sigmoid(a) * b).astype(BF16)
        y = x + MACARON_WEIGHT * _dot(act_ref[tok, :], w_out_ref[...])
        if extra == "final":
            y = _rmsnorm(y, gn_ref[...])
        o_ref[tok, :] = y
        if extra == "s5":
            u_ref, = extra_refs
            yn = _rmsnorm(y, gn_ref[...])
            for c in range(per_sub):
                _chunk_slab(u_ref, sub * per_sub + c)[...] = yn[c * S5_CHUNK:(c + 1) * S5_CHUNK, :].T


def _ffn(x2d, g, w_in, w_out, layer, gn, extra=None):
    tokens = x2d.shape[0]
    row = pl.BlockSpec((FFN_TILE, D_MODEL), lambda i: (i, 0))
    out_specs = [row]
    out_shape = [jax.ShapeDtypeStruct((tokens, D_MODEL), F32)]
    if extra == "s5":
        per_tile = FFN_TILE // S5_CHUNK
        out_specs.append(pl.BlockSpec((D_MODEL, per_tile, S5_CHUNK), lambda i: (0, i, 0)))
        out_shape.append(jax.ShapeDtypeStruct((D_MODEL, tokens // S5_CHUNK, S5_CHUNK), F32))
    return pl.pallas_call(
        functools.partial(_ffn_kernel, extra=extra),
        grid=(tokens // FFN_TILE,),
        in_specs=[row, _resident((1, D_MODEL)), _layer_resident((D_MODEL, 2 * D_FF), layer),
                  _layer_resident((D_FF, D_MODEL), layer), _resident((1, D_MODEL))],
        out_specs=out_specs,
        out_shape=out_shape,
        scratch_shapes=[pltpu.VMEM((FFN_TILE, D_FF), BF16)],
        compiler_params=_params(1),
        name="ffn",
    )(x2d, g, w_in, w_out, gn)


def _kv_kernel(mem_ref, g_ref, w_ref, o_ref):
    mn = _rmsnorm(mem_ref[...], g_ref[...]).astype(BF16)
    o_ref[0] = _dot(mn, w_ref[0]).astype(BF16)


def _memory_kv(mem2d, g, w_kv):
    rows = mem2d.shape[0]
    half = rows // 2
    return pl.pallas_call(
        _kv_kernel,
        grid=(DEPTH, 2),
        in_specs=[pl.BlockSpec((half, D_MODEL), lambda l, r: (r, 0)),
                  _resident((1, D_MODEL)),
                  pl.BlockSpec((1, D_MODEL, 2 * D_MODEL), lambda l, r: (l, 0, 0))],
        out_specs=pl.BlockSpec((1, half, 2 * D_MODEL), lambda l, r: (l, r, 0)),
        out_shape=jax.ShapeDtypeStruct((DEPTH, rows, 2 * D_MODEL), BF16),
        compiler_params=_params(2),
        name="memory_kv",
    )(mem2d, g, w_kv)


def _xattn_kernel(x_ref, g_ref, wq_ref, k_ref, v_ref, wo_ref, o_ref):
    for sub in range(XATTN_TILE // TOKEN_TILE):
        rows = slice(sub * TOKEN_TILE, (sub + 1) * TOKEN_TILE)
        x = x_ref[0, rows, :]
        xn = _rmsnorm(x, g_ref[...]).astype(BF16)
        q = (_dot(xn, wq_ref[...]) * (XATTN_DIM ** -0.5)).astype(BF16)
        heads = []
        for h in range(XATTN_HEADS):
            sl = slice(h * XATTN_DIM, (h + 1) * XATTN_DIM)
            s = lax.dot_general(q[:, sl], k_ref[0, :, sl], (((1,), (1,)), ((), ())),
                                preferred_element_type=F32)
            e = jnp.exp(s - jnp.max(s, axis=-1, keepdims=True))
            p = e / jnp.sum(e, axis=-1, keepdims=True)
            heads.append(_dot(p.astype(BF16), v_ref[0, :, sl]).astype(BF16))
        o = jnp.concatenate(heads, axis=-1)
        o_ref[0, rows, :] = x + _dot(o, wo_ref[...])


def _xattn(x3d, g, w_q, kv, w_o, layer):
    bsz, seq, _ = x3d.shape
    row = pl.BlockSpec((1, XATTN_TILE, D_MODEL), lambda b, i: (b, i, 0))
    return pl.pallas_call(
        _xattn_kernel,
        grid=(bsz, seq // XATTN_TILE),
        in_specs=[row, _resident((1, D_MODEL)), _layer_resident((D_MODEL, D_MODEL), layer),
                  pl.BlockSpec((None, 1, N_MEM, D_MODEL), lambda b, i: (layer, b, 0, 0)),
                  pl.BlockSpec((None, 1, N_MEM, D_MODEL), lambda b, i: (layer, b, 0, 1)),
                  _layer_resident((D_MODEL, D_MODEL), layer)],
        out_specs=row,
        out_shape=jax.ShapeDtypeStruct(x3d.shape, F32),
        compiler_params=_params(2),
        name="xattn",
    )(x3d, g, w_q, kv, kv, w_o)


def _ret_decay_tables():
    c = RET_CHUNK
    log_gamma = jnp.log(1.0 - jnp.exp2(-5.0 - jnp.arange(RET_HEADS, dtype=F32)))
    idx = jnp.arange(c)
    diff = idx[:, None] - idx[None, :]
    d_mat = jnp.where(diff >= 0, jnp.exp(log_gamma[:, None, None]
                                         * jnp.maximum(diff, 0).astype(F32)), 0.0)
    xi = jnp.exp(log_gamma[:, None] * (idx + 1).astype(F32))
    zeta = jnp.exp(log_gamma[:, None] * (c - 1 - idx).astype(F32))
    xi = jnp.broadcast_to(xi[:, :, None], (RET_HEADS, c, LANES))
    zeta = jnp.broadcast_to(zeta.T[:, :, None], (c, RET_HEADS, RET_QK_DIM)).reshape(c, D_MODEL)
    return d_mat, xi, zeta


def _ret_in_kernel(x_ref, g_ref, rope_a_ref, rope_b_ref, zeta_ref, w_ref,
                   q_ref, k_ref, kz_ref, v_ref, sg_ref, *rope_out_refs, make_rope):
    if make_rope:
        cos_ref, sin_ref = rope_out_refs
        ang = rope_a_ref[...] * rope_b_ref[...]
        cos_ref[...] = jnp.cos(ang)
        sin_ref[...] = jnp.sin(ang)
    else:
        cos_ref, sin_ref = rope_a_ref, rope_b_ref
    k_scale = RET_QK_DIM ** -0.5
    for sub in range(RET_IN_TILE // TOKEN_TILE):
        base = sub * TOKEN_TILE
        tok = slice(base, base + TOKEN_TILE)
        xn = _rmsnorm(x_ref[tok, :], g_ref[...]).astype(BF16)
        cos = cos_ref[tok, :]
        sin = sin_ref[tok, :]
        cos_k, sin_k = cos * k_scale, sin * k_scale
        for hd in range(RET_HEADS):
            lo = hd * RET_QK_DIM
            mid = lo + ROPE_HALF
            hi = lo + RET_QK_DIM
            qh = _dot(xn, w_ref[:, lo:hi])
            q1, q2 = qh[:, :ROPE_HALF], qh[:, ROPE_HALF:]
            q_ref[tok, lo:mid] = (q1 * cos - q2 * sin).astype(BF16)
            q_ref[tok, mid:hi] = (q1 * sin + q2 * cos).astype(BF16)
            kh = _dot(xn, w_ref[:, D_MODEL + lo:D_MODEL + hi])
            k1, k2 = kh[:, :ROPE_HALF], kh[:, ROPE_HALF:]
            for dst, kr in ((slice(lo, mid), k1 * cos_k - k2 * sin_k),
                            (slice(mid, hi), k1 * sin_k + k2 * cos_k)):
                k_ref[tok, dst] = kr.astype(BF16)
                for ci in range(TOKEN_TILE // RET_CHUNK):
                    rows = slice(ci * RET_CHUNK, (ci + 1) * RET_CHUNK)
                    kz_ref[base + ci * RET_CHUNK:base + (ci + 1) * RET_CHUNK, dst] = (
                        kr[rows, :] * zeta_ref[:, dst]).astype(BF16)
        for blk in range(RET_VALUE_WIDTH // RET_QK_DIM):
            cols = slice(blk * RET_QK_DIM, (blk + 1) * RET_QK_DIM)
            v_ref[tok, cols] = _dot(xn, w_ref[:, 2 * D_MODEL + blk * RET_QK_DIM:
                                              2 * D_MODEL + (blk + 1) * RET_QK_DIM]).astype(BF16)
            gate = _dot(xn, w_ref[:, 2 * D_MODEL + RET_VALUE_WIDTH + blk * RET_QK_DIM:
                                  2 * D_MODEL + RET_VALUE_WIDTH + (blk + 1) * RET_QK_DIM])
            sg_ref[tok, cols] = (gate * jax.nn.sigmoid(gate)).astype(BF16)


def _ret_in(x2d, g, rope, zeta, w_in, layer, make_rope):
    tokens = x2d.shape[0]

    def row(width):
        return pl.BlockSpec((RET_IN_TILE, width), lambda i: (i, 0))

    def out(width, dtype=BF16):
        return jax.ShapeDtypeStruct((tokens, width), dtype)

    rope_b = _resident((1, ROPE_HALF)) if make_rope else row(ROPE_HALF)
    rope_out = 2 if make_rope else 0
    return pl.pallas_call(
        functools.partial(_ret_in_kernel, make_rope=make_rope),
        grid=(tokens // RET_IN_TILE,),
        in_specs=[row(D_MODEL), _resident((1, D_MODEL)), row(ROPE_HALF), rope_b,
                  _resident((RET_CHUNK, D_MODEL)), _layer_resident((D_MODEL, RET_IN_WIDTH), layer)],
        out_specs=[row(D_MODEL), row(D_MODEL), row(D_MODEL), row(RET_VALUE_WIDTH),
                   row(RET_VALUE_WIDTH)] + [row(ROPE_HALF)] * rope_out,
        out_shape=[out(D_MODEL), out(D_MODEL), out(D_MODEL), out(RET_VALUE_WIDTH),
                   out(RET_VALUE_WIDTH)] + [out(ROPE_HALF, F32)] * rope_out,
        compiler_params=_params(1),
        name="ret_in",
    )(x2d, g, *rope, zeta, w_in)


def _ret_core_kernel(x_ref, q_ref, k_ref, kz_ref, v_ref, sg_ref, d_ref, xi_ref, w_out_ref, o_ref,
                     state_ref, y_ref):
    @pl.when(pl.program_id(1) == 0)
    def _():
        state_ref[...] = jnp.zeros_like(state_ref)

    c = RET_CHUNK
    for hd in range(RET_HEADS):
        gamma_chunk = (1.0 - 2.0 ** (-5 - hd)) ** c
        xi = jnp.concatenate([xi_ref[hd]] * (RET_V_DIM // LANES), axis=-1)
        qk = slice(hd * RET_QK_DIM, (hd + 1) * RET_QK_DIM)
        vv = slice(hd * RET_V_DIM, (hd + 1) * RET_V_DIM)
        for ci in range(TOKEN_TILE // c):
            rows = slice(ci * c, (ci + 1) * c)
            qc = q_ref[0, rows, qk]
            vc = v_ref[0, rows, vv]
            state = state_ref[hd]
            inner = lax.dot_general(qc, k_ref[0, rows, qk], (((1,), (1,)), ((), ())),
                                    preferred_element_type=F32) * d_ref[hd]
            o = _dot(inner.astype(BF16), vc) + _dot(qc, state.astype(BF16)) * xi
            state_ref[hd] = state * gamma_chunk + lax.dot_general(
                kz_ref[0, rows, qk], vc, (((0,), (0,)), ((), ())), preferred_element_type=F32)
            mu = jnp.mean(o, axis=-1, keepdims=True)
            dev = o - mu
            var = jnp.mean(dev * dev, axis=-1, keepdims=True)
            y = sg_ref[0, rows, vv].astype(F32) * (dev * lax.rsqrt(var + EPS))
            y_ref[rows, vv] = y.astype(BF16)
    o_ref[0] = x_ref[0] + _dot(y_ref[...], w_out_ref[...])


def _ret_core(x3d, q, k, kz, v, sg, d_mat, xi, w_out, layer):
    bsz, seq, _ = x3d.shape

    def row(width):
        return pl.BlockSpec((1, TOKEN_TILE, width), lambda b, i: (b, i, 0))

    return pl.pallas_call(
        _ret_core_kernel,
        grid=(bsz, seq // TOKEN_TILE),
        in_specs=[row(D_MODEL), row(D_MODEL), row(D_MODEL), row(D_MODEL), row(RET_VALUE_WIDTH),
                  row(RET_VALUE_WIDTH), _resident((RET_HEADS, RET_CHUNK, RET_CHUNK)),
                  _resident((RET_HEADS, RET_CHUNK, LANES)),
                  _layer_resident((RET_VALUE_WIDTH, D_MODEL), layer)],
        out_specs=row(D_MODEL),
        out_shape=jax.ShapeDtypeStruct(x3d.shape, F32),
        scratch_shapes=[pltpu.VMEM((RET_HEADS, RET_QK_DIM, RET_V_DIM), F32),
                        pltpu.VMEM((TOKEN_TILE, RET_VALUE_WIDTH), BF16)],
        compiler_params=_params(2),
        name="ret_core",
    )(x3d, q, k, kz, v, sg, d_mat, xi, w_out)


def _s5_operator_kernel(rows_ref, cols_ref, b_ref, c_ref, ct_ref, w_ref, z_ref, r_ref, a_ref, m_ref):
    n = S5_CHUNK
    lane = lax.broadcasted_iota(jnp.int32, (1, 2 * S5_STATE), 1)
    sign = jnp.where(lane < S5_STATE, -1.0, 1.0)
    lr, li = rows_ref[0, 0:1, :], rows_ref[0, 1:2, :]
    dt = jnp.exp(rows_ref[0, 2:3, :])
    mag = jnp.exp(lr * dt)
    ar = mag * jnp.cos(li * dt)
    ai = mag * jnp.sin(li * dt)
    den = lr * lr + li * li
    nr = ar - 1.0
    zr = (nr * lr + ai * li) / den
    zi = (ai * lr - nr * li) / den
    bb_ri = zr * b_ref[0, 0] + zi * b_ref[0, 1]
    bb_ir = zr * b_ref[0, 2] + zi * b_ref[0, 3]
    bb_rr = jnp.where(lane < S5_STATE, bb_ri, bb_ir)
    bb_ii = jnp.where(lane < S5_STATE, bb_ir, bb_ri)

    full_mag = jnp.exp(lr * dt * n)
    alr = full_mag * jnp.cos(li * dt * n)
    ali = full_mag * jnp.sin(li * dt * n) * sign
    row = lax.broadcasted_iota(jnp.int32, (8, 2 * S5_STATE), 0)
    a_ref[0] = jnp.where(row == 0, alr, jnp.where(row == 1, ali, jnp.where(row == 2, -ali, 0.0)))

    back = (n - 1 - lax.broadcasted_iota(jnp.int32, (n, 2 * S5_STATE), 0)).astype(F32)
    back_mag = jnp.exp(lr * dt * back)
    pw_r = back_mag * jnp.cos(li * dt * back)
    pw_i = back_mag * jnp.sin(li * dt * back)
    for hi in range(S5_GROUP):
        sl = slice(hi, hi + 1)
        rows = slice(hi * n, (hi + 1) * n)
        w_ref[0, rows, :2 * S5_STATE] = (pw_r * bb_ri[sl] + pw_i * (bb_ir[sl] * sign)).astype(BF16)
        w_ref[0, rows, 2 * S5_STATE:] = (pw_r * bb_ir[sl] - pw_i * (bb_ri[sl] * sign)).astype(BF16)
        m_ref[hi * S5_GROUP:(hi + 1) * S5_GROUP, :] = bb_rr[sl] * c_ref[0, 0] + bb_ii[sl] * c_ref[0, 1]

    fwd = (lax.broadcasted_iota(jnp.int32, (2 * S5_STATE, n), 1) + 1).astype(F32)
    fwd_mag = jnp.exp(cols_ref[0, 0] * dt * fwd)
    p1_r = fwd_mag * jnp.cos(cols_ref[0, 1] * dt * fwd)
    p1_i = fwd_mag * jnp.sin(cols_ref[0, 1] * dt * fwd)
    for ho in range(S5_GROUP):
        z_ref[0, :, ho * n:(ho + 1) * n] = (ct_ref[0, 0][:, ho:ho + 1] * p1_r
                                            + ct_ref[0, 1][:, ho:ho + 1] * p1_i).astype(BF16)
    first = lax.broadcasted_iota(jnp.int32, (2 * S5_STATE, n), 1) == 0
    p0_r = jnp.where(first, 1.0, pltpu.roll(p1_r, 1, 1))
    p0_i = jnp.where(first, 0.0, pltpu.roll(p1_i, 1, 1))
    top = lax.broadcasted_iota(jnp.int32, (2 * S5_STATE, n), 0) < S5_STATE
    r_ref[0] = jnp.dot(m_ref[...], jnp.where(top, p0_r, p0_i), preferred_element_type=F32,
                       precision=lax.Precision.HIGHEST)


def _s5_operators(lam_re, lam_im, log_dt, b_re, b_im, c_re, c_im):
    def pair(u, v):
        return jnp.concatenate([u, v], axis=-1)

    lr, li = lam_re.astype(F32), lam_im.astype(F32)
    lam = jnp.stack([pair(lr, lr), pair(li, li)], axis=1)
    ldt = jnp.broadcast_to(log_dt.astype(F32)[:, None, None], (S5_GROUPS, 1, 2 * S5_STATE))
    rows = jnp.concatenate([lam, ldt, jnp.zeros((S5_GROUPS, 5, 2 * S5_STATE), F32)], axis=1)
    cols = jnp.broadcast_to(lam[..., None], (S5_GROUPS, 2, 2 * S5_STATE, S5_CHUNK))
    br, bi = b_re.astype(F32).transpose(0, 2, 1), b_im.astype(F32).transpose(0, 2, 1)
    b = jnp.stack([pair(br, bi), pair(-bi, br), pair(bi, br), pair(br, -bi)], axis=1)
    cr, ci = c_re.astype(F32), c_im.astype(F32)
    c = jnp.stack([pair(cr, -ci), pair(-ci, -cr)], axis=1)
    ct = c.transpose(0, 1, 3, 2)

    def per_group(*shape):
        return pl.BlockSpec((1,) + shape, lambda g: (g,) + (0,) * len(shape))

    w, z, resp, a = pl.pallas_call(
        _s5_operator_kernel,
        grid=(S5_GROUPS,),
        in_specs=[per_group(8, 2 * S5_STATE), per_group(2, 2 * S5_STATE, S5_CHUNK),
                  per_group(4, S5_GROUP, 2 * S5_STATE), per_group(2, S5_GROUP, 2 * S5_STATE),
                  per_group(2, 2 * S5_STATE, S5_GROUP)],
        out_specs=[per_group(S5_COLS, 4 * S5_STATE), per_group(2 * S5_STATE, S5_COLS),
                   per_group(S5_GROUP * S5_GROUP, S5_CHUNK), per_group(8, 2 * S5_STATE)],
        out_shape=[jax.ShapeDtypeStruct((S5_GROUPS, S5_COLS, 4 * S5_STATE), BF16),
                   jax.ShapeDtypeStruct((S5_GROUPS, 2 * S5_STATE, S5_COLS), BF16),
                   jax.ShapeDtypeStruct((S5_GROUPS, S5_GROUP * S5_GROUP, S5_CHUNK), F32),
                   jax.ShapeDtypeStruct((S5_GROUPS, 8, 2 * S5_STATE), F32)],
        scratch_shapes=[pltpu.VMEM((S5_GROUP * S5_GROUP, 2 * S5_STATE), F32)],
        compiler_params=_params(1),
        name="s5_operators",
    )(rows, cols, b, c, ct)
    return resp, w, z, a


def _toeplitz_block(r_ref, row, causal):
    n = S5_CHUNK
    lags = jnp.broadcast_to(r_ref[0, pl.ds(row, 1), :], (n, n))
    blk = pltpu.roll(lags, 0, 1, stride=1, stride_axis=0)
    return jnp.where(causal, blk, 0.0).astype(BF16)


def _s5_core_kernel(u_ref, r0_ref, r_next_ref, w_ref, z_ref, a_ref, y_ref, t0_ref, t1_ref, us_ref,
                    v_ref, xin_ref, *, bsz, n_chunks):
    n = S5_CHUNK
    g = pl.program_id(0)
    causal = (lax.broadcasted_iota(jnp.int32, (n, n), 1)
              >= lax.broadcasted_iota(jnp.int32, (n, n), 0))

    @pl.when(g == 0)
    def _():
        def build_rows(hi, carry):
            for ho in range(S5_GROUP):
                t0_ref[pl.ds(pl.multiple_of(hi * n, n), n), ho * n:(ho + 1) * n] = (
                    _toeplitz_block(r0_ref, hi * S5_GROUP + ho, causal))
            return carry

        lax.fori_loop(0, S5_GROUP, build_rows, 0)

    @pl.when(g % 2 == 0)
    def _():
        _s5_group(u_ref, r_next_ref, w_ref, z_ref, a_ref, y_ref, t0_ref, t1_ref, us_ref, v_ref,
                  xin_ref, causal, bsz, n_chunks)

    @pl.when(g % 2 == 1)
    def _():
        _s5_group(u_ref, r_next_ref, w_ref, z_ref, a_ref, y_ref, t1_ref, t0_ref, us_ref, v_ref,
                  xin_ref, causal, bsz, n_chunks)


def _s5_group(u_ref, r_next_ref, w_ref, z_ref, a_ref, y_ref, t_ref, t_next_ref, us_ref, v_ref,
              xin_ref, causal, bsz, n_chunks):
    n = S5_CHUNK
    for hi in range(S5_GROUP):
        for ho in range(S5_GROUP):
            t_next_ref[hi * n:(hi + 1) * n, ho * n:(ho + 1) * n] = (
                _toeplitz_block(r_next_ref, hi * S5_GROUP + ho, causal))

    for hi in range(S5_GROUP):
        us_ref[:, hi * n:(hi + 1) * n] = u_ref[hi].astype(BF16)
    u = us_ref[...]
    half = 2 * S5_STATE
    v = _dot(u, w_ref[0])
    v_ref[0] = v[:, :half]
    v_ref[1] = v[:, half:]
    a_same = a_ref[0, 0:1, :]
    a_cross_p = a_ref[0, 1:2, :]
    a_cross_q = a_ref[0, 2:3, :]
    p = jnp.zeros((bsz, half), F32)
    q = jnp.zeros((bsz, half), F32)
    for c in range(n_chunks):
        xin_ref[pl.ds(c, bsz, stride=n_chunks), :] = p
        p, q = (a_same * p + a_cross_p * q + v_ref[0, pl.ds(c, bsz, stride=n_chunks), :],
                a_same * q + a_cross_q * p + v_ref[1, pl.ds(c, bsz, stride=n_chunks), :])
    xin = xin_ref[...].astype(BF16)
    for pair in range(S5_GROUP // 2):
        cols = slice(2 * pair * n, (2 * pair + 2) * n)
        y = _dot(u, t_ref[:, cols]) + _dot(xin, z_ref[0, :, cols])
        y_ref[2 * pair] = y[:, :n]
        y_ref[2 * pair + 1] = y[:, n:]


def _s5_core(u3d, resp, w, z, a, bsz):
    rows = u3d.shape[1]
    chunk_rows = pl.BlockSpec((S5_GROUP, rows, S5_CHUNK), lambda g: (g, 0, 0))
    return pl.pallas_call(
        functools.partial(_s5_core_kernel, bsz=bsz, n_chunks=rows // bsz),
        grid=(S5_GROUPS,),
        in_specs=[chunk_rows,
                  pl.BlockSpec((1, S5_GROUP * S5_GROUP, S5_CHUNK), lambda g: (0, 0, 0)),
                  pl.BlockSpec((1, S5_GROUP * S5_GROUP, S5_CHUNK),
                               lambda g: (jnp.minimum(g + 1, S5_GROUPS - 1), 0, 0)),
                  pl.BlockSpec((1, S5_COLS, 4 * S5_STATE), lambda g: (g, 0, 0)),
                  pl.BlockSpec((1, 2 * S5_STATE, S5_COLS), lambda g: (g, 0, 0)),
                  pl.BlockSpec((1, 8, 2 * S5_STATE), lambda g: (g, 0, 0))],
        out_specs=chunk_rows,
        out_shape=jax.ShapeDtypeStruct((D_MODEL, rows, S5_CHUNK), F32),
        scratch_shapes=[pltpu.VMEM((S5_COLS, S5_COLS), BF16),
                        pltpu.VMEM((S5_COLS, S5_COLS), BF16),
                        pltpu.VMEM((rows, S5_COLS), BF16),
                        pltpu.VMEM((2, rows, 2 * S5_STATE), F32),
                        pltpu.VMEM((rows, 2 * S5_STATE), F32)],
        compiler_params=_params(1),
        name="s5_core",
    )(u3d, resp, resp, w, z, a)


def _gelu_tanh(x):
    return 0.5 * x * (1.0 + jnp.tanh(math.sqrt(2.0 / math.pi) * (x + 0.044715 * (x * x * x))))


def _s5_post_kernel(x_ref, y_ref, g_ref, d_ref, w_ref, o_ref, act_ref):
    per_sub = TOKEN_TILE // S5_CHUNK
    for sub in range(S5_TILE_CHUNKS // per_sub):
        tok = slice(sub * TOKEN_TILE, (sub + 1) * TOKEN_TILE)
        x = x_ref[0, tok, :]
        skip = d_ref[...] * _rmsnorm(x, g_ref[...])
        for c in range(per_sub):
            rows = slice(sub * TOKEN_TILE + c * S5_CHUNK, sub * TOKEN_TILE + (c + 1) * S5_CHUNK)
            act_ref[rows, :] = _gelu_tanh(_chunk_slab(y_ref, sub * per_sub + c)[...].T
                                          + skip[c * S5_CHUNK:(c + 1) * S5_CHUNK, :]).astype(BF16)
        act = act_ref[tok, :]
        for blk in range(D_MODEL // GLU_BLOCK):
            cols = slice(blk * GLU_BLOCK, (blk + 1) * GLU_BLOCK)
            a = _dot(act, w_ref[:, cols])
            gate = _dot(act, w_ref[:, D_MODEL + blk * GLU_BLOCK:D_MODEL + (blk + 1) * GLU_BLOCK])
            o_ref[0, tok, cols] = x[:, cols] + a * jax.nn.sigmoid(gate)


def _s5_post(x3d, y3d, g, d_skip, w_glu, layer):
    bsz, seq, _ = x3d.shape
    tile = S5_TILE_CHUNKS * S5_CHUNK
    steps = seq // tile
    row = pl.BlockSpec((1, tile, D_MODEL), lambda b, i: (b, i, 0))
    return pl.pallas_call(
        _s5_post_kernel,
        grid=(bsz, steps),
        in_specs=[row,
                  pl.BlockSpec((D_MODEL, S5_TILE_CHUNKS, S5_CHUNK), lambda b, i: (0, b * steps + i, 0)),
                  _resident((1, D_MODEL)), _resident((1, D_MODEL)),
                  _layer_resident((D_MODEL, 2 * D_MODEL), layer)],
        out_specs=row,
        out_shape=jax.ShapeDtypeStruct(x3d.shape, F32),
        scratch_shapes=[pltpu.VMEM((tile, D_MODEL), BF16)],
        compiler_params=_params(2),
        name="s5_post",
    )(x3d, y3d, g, d_skip, w_glu)


def kernel(x, mem, positions, norm_gains, mem_norm, final_norm, ffn1_w_in, ffn1_w_out, ffn2_w_in, ffn2_w_out, s5_lam_re, s5_lam_im, s5_log_dt, s5_b_re, s5_b_im, s5_c_re, s5_c_im, s5_d, s5_w_glu, ret_w_in, ret_w_out, xattn_w_q, xattn_w_kv, xattn_w_o):
    bsz, seq, _ = x.shape
    tokens = bsz * seq
    gains = norm_gains.astype(F32)
    gf = final_norm.astype(F32).reshape(1, D_MODEL)

    kv = _memory_kv(mem.reshape(bsz * N_MEM, D_MODEL), mem_norm.astype(F32).reshape(1, D_MODEL),
                    xattn_w_kv.astype(BF16))
    kv = kv.reshape(DEPTH, bsz, N_MEM, 2 * D_MODEL)
    pos_lanes = jnp.broadcast_to(positions.astype(F32).reshape(tokens, 1), (tokens, ROPE_HALF))
    inv_freq = (1.0 / (ROPE_BASE ** jnp.linspace(0.0, 1.0, ROPE_HALF, dtype=F32))).reshape(1, ROPE_HALF)
    d_mat, xi, zeta = _ret_decay_tables()
    rope_tables = None
    ffn1_w, ffn2_w = (ffn1_w_in.astype(BF16), ffn1_w_out.astype(BF16)), \
                     (ffn2_w_in.astype(BF16), ffn2_w_out.astype(BF16))
    ret_w_in, ret_w_out, s5_w_glu = ret_w_in.astype(BF16), ret_w_out.astype(BF16), s5_w_glu.astype(BF16)
    xattn_w_q, xattn_w_o = xattn_w_q.astype(BF16), xattn_w_o.astype(BF16)

    for i in range(DEPTH):
        g = [gains[i, j].reshape(1, D_MODEL) for j in range(4)]
        ffn1 = functools.partial(_ffn, x.reshape(tokens, D_MODEL), g[0], *ffn1_w, i)
        j = i // 2
        if i % 2 == 0:
            resp, w, z, a = _s5_operators(s5_lam_re[j], s5_lam_im[j], s5_log_dt[j], s5_b_re[j],
                                          s5_b_im[j], s5_c_re[j], s5_c_im[j])
            x2d, u = ffn1(g[1], "s5")
            x = x2d.reshape(bsz, seq, D_MODEL)
            y = _s5_core(u, resp, w, z, a, bsz)
            x = _s5_post(x, y, g[1], s5_d[j].astype(F32).reshape(1, D_MODEL), s5_w_glu, j)
        else:
            x2d, = ffn1(gf)
            if rope_tables is None:
                *heads, cos, sin = _ret_in(x2d, g[1], (pos_lanes, inv_freq), zeta, ret_w_in, j, True)
                rope_tables = (cos, sin)
            else:
                heads = _ret_in(x2d, g[1], rope_tables, zeta, ret_w_in, j, False)
            q, k, kz, v, sg = [t.reshape(bsz, seq, t.shape[-1]) for t in heads]
            x = _ret_core(x2d.reshape(bsz, seq, D_MODEL), q, k, kz, v, sg, d_mat, xi, ret_w_out, j)
        x = _xattn(x, g[2], xattn_w_q, kv, xattn_w_o, i)
        x, = _ffn(x.reshape(tokens, D_MODEL), g[3], *ffn2_w, i, gf,
                  "final" if i == DEPTH - 1 else None)
        x = x.reshape(bsz, seq, D_MODEL)
    return x
```
